```python
import math
import jax, jax.numpy as jnp
from jax import lax
import numpy as np

D_MODEL = 1024
BATCH = 32
SEQ = 2048
DEPTH = 2

N_META = 16
Q_BLOCK = 128
GROUP_HEAD_DIM = 64
D_MIX = D_MODEL
ATTN_HEADS = 8
ATTN_HEAD_DIM = 64
D_ATTN = ATTN_HEADS * ATTN_HEAD_DIM
D_CONV = D_MIX // 4
D_SC = D_MIX - D_ATTN - D_CONV
CONV_KERNEL = 31
SC_KERNEL = 3
D_FF = 2816
FFN_KERNEL = 3
RMS_EPS = 1e-6
LN_EPS = 1e-5
IN_SPLITS = (D_ATTN, D_ATTN, D_ATTN, ATTN_HEADS,
             D_CONV, D_CONV,
             D_SC, D_SC, D_SC)
D_IN = sum(IN_SPLITS)
NEG_INF = -1e30

kernel_name = "hymba_style_fox_conformer_shortconv_hybrid"


def rms_norm(x, g):
    xf = x.astype(jnp.float32)
    y = xf * lax.rsqrt(jnp.mean(xf * xf, axis=-1, keepdims=True) + RMS_EPS)
    return (y * g.astype(jnp.float32)).astype(x.dtype)


def head_rms_norm(y, g):
    b, l, d = y.shape
    yf = y.astype(jnp.float32).reshape(b, l, d // GROUP_HEAD_DIM, GROUP_HEAD_DIM)
    yf = yf * lax.rsqrt(jnp.mean(yf * yf, axis=-1, keepdims=True) + RMS_EPS)
    return (yf.reshape(b, l, d) * g.astype(jnp.float32)).astype(y.dtype)


def layer_norm(x, g, b):
    xf = x.astype(jnp.float32)
    mu = jnp.mean(xf, axis=-1, keepdims=True)
    var = jnp.mean(jnp.square(xf - mu), axis=-1, keepdims=True)
    y = (xf - mu) * lax.rsqrt(var + LN_EPS)
    return (y * g.astype(jnp.float32) + b.astype(jnp.float32)).astype(x.dtype)


def causal_depthwise_conv(x, w):
    k, c = w.shape
    return lax.conv_general_dilated(
        x, w[:, None, :].astype(x.dtype), window_strides=(1,),
        padding=[(k - 1, 0)], dimension_numbers=("NWC", "WIO", "NWC"),
        feature_group_count=c)


def forgetting_attention(q, k, v, log_f):
    L = q.shape[1]
    scale = ATTN_HEAD_DIM ** -0.5
    F = jnp.cumsum(log_f, axis=1).transpose(0, 2, 1)

    def attend(start, end):
        qb = q[:, start:end]
        s = jnp.einsum("bqhd,bkhd->bhqk", qb, k[:, :end],
                       preferred_element_type=jnp.float32) * scale
        s = s + F[:, :, start:end, None] - F[:, :, None, :end]
        mask = jnp.arange(start, end)[:, None] >= jnp.arange(end)[None, :]
        s = jnp.where(mask, s, NEG_INF)
        p = jax.nn.softmax(s, axis=-1)
        return jnp.einsum("bhqk,bkhd->bqhd", p.astype(v.dtype), v[:, :end])

    outs = [attend(0, N_META)]
    for i in range((L - N_META) // Q_BLOCK):
        start = N_META + i * Q_BLOCK
        outs.append(attend(start, start + Q_BLOCK))
    return jnp.concatenate(outs, axis=1)


def mixer_sublayer(h, pre_g, post_g, w_in, b_forget, a_dw_w, a_dw_b, a_ln_g,
                   a_ln_b, a_pw_w, sc_conv_w, head_g, w_out):
    bsz, L, _ = h.shape
    u = rms_norm(h, pre_g)
    z = jnp.einsum("bld,de->ble", u, w_in)
    idx = np.cumsum(IN_SPLITS)[:-1].tolist()
    q, k, v, f_logit, a_val, a_gate, sc_b, sc_c, sc_x = jnp.split(z, idx, axis=-1)

    log_f = jax.nn.log_sigmoid((f_logit + b_forget).astype(jnp.float32))
    hs = (bsz, L, ATTN_HEADS, ATTN_HEAD_DIM)
    y_attn = forgetting_attention(q.reshape(hs), k.reshape(hs), v.reshape(hs), log_f)
    y_attn = y_attn.reshape(bsz, L, D_ATTN)

    a = a_val * jax.nn.sigmoid(a_gate)
    a = causal_depthwise_conv(a, a_dw_w) + a_dw_b
    a = jax.nn.silu(layer_norm(a, a_ln_g, a_ln_b))
    y_conv = jnp.einsum("blc,ce->ble", a, a_pw_w)

    y_sc = sc_b * causal_depthwise_conv(sc_c * sc_x, sc_conv_w)

    y = jnp.concatenate([y_attn, y_conv, y_sc], axis=-1)
    y = jnp.einsum("ble,ed->bld", head_rms_norm(y, head_g), w_out)
    return h + rms_norm(y, post_g)


def ffn_sublayer(h, pre_g, post_g, w_up, conv_w, w_down):
    u = rms_norm(h, pre_g)
    z = jnp.einsum("bld,df->blf", u, w_up)
    z = causal_depthwise_conv(z, conv_w)
    g, up = jnp.split(z, 2, axis=-1)
    y = jnp.einsum("blf,fd->bld", jax.nn.silu(g) * up, w_down)
    return h + rms_norm(y, post_g)


def setup_inputs(seed: int = 0) -> dict:
    key = jax.random.key(seed)
    ks = jax.random.split(key, 20)
    f32 = jnp.float32

    def nrm(k, shape, scale):
        return jax.random.normal(k, shape, f32) * scale

    def gain(k, shape):
        return 1.0 + 0.05 * jax.random.normal(k, shape, f32)

    x = jax.random.normal(ks[0], (BATCH, SEQ, D_MODEL), f32)
    meta_tokens = nrm(ks[1], (N_META, D_MODEL), 1.0)
    col_scale = np.ones((D_IN,), np.float32)
    f0 = 3 * D_ATTN
    col_scale[f0:f0 + ATTN_HEADS] = 0.1
    w_in = nrm(ks[2], (DEPTH, D_MODEL, D_IN), D_MODEL ** -0.5) * jnp.asarray(col_scale)
    b_forget = jax.random.uniform(ks[3], (DEPTH, ATTN_HEADS), f32, 3.0, 6.0)
    return {
        "x": x,
        "meta_tokens": meta_tokens,
        "mix_pre_g": gain(ks[4], (DEPTH, D_MODEL)),
        "mix_post_g": gain(ks[5], (DEPTH, D_MODEL)),
        "w_in": w_in,
        "b_forget": b_forget,
        "a_dw_w": nrm(ks[6], (DEPTH, CONV_KERNEL, D_CONV), CONV_KERNEL ** -0.5),
        "a_dw_b": nrm(ks[7], (DEPTH, D_CONV), 0.02),
        "a_ln_g": gain(ks[8], (DEPTH, D_CONV)),
        "a_ln_b": nrm(ks[9], (DEPTH, D_CONV), 0.02),
        "a_pw_w": nrm(ks[10], (DEPTH, D_CONV, D_CONV), D_CONV ** -0.5),
        "sc_conv_w": nrm(ks[11], (DEPTH, SC_KERNEL, D_SC), SC_KERNEL ** -0.5),
        "head_g": gain(ks[12], (DEPTH, D_MIX)),
        "w_out": nrm(ks[13], (DEPTH, D_MIX, D_MODEL), D_MIX ** -0.5),
        "ffn_pre_g": gain(ks[14], (DEPTH, D_MODEL)),
        "ffn_post_g": gain(ks[15], (DEPTH, D_MODEL)),
        "ffn_w_up": nrm(ks[16], (DEPTH, D_MODEL, 2 * D_FF), D_MODEL ** -0.5),
        "ffn_conv_w": nrm(ks[17], (DEPTH, FFN_KERNEL, 2 * D_FF), FFN_KERNEL ** -0.5),
        "ffn_w_down": nrm(ks[18], (DEPTH, D_FF, D_MODEL), D_FF ** -0.5),
    }


def reference(x, meta_tokens, mix_pre_g, mix_post_g, w_in, b_forget, a_dw_w,
              a_dw_b, a_ln_g, a_ln_b, a_pw_w, sc_conv_w, head_g, w_out,
              ffn_pre_g, ffn_post_g, ffn_w_up, ffn_conv_w, ffn_w_down):
    bsz = x.shape[0]
    meta = jnp.broadcast_to(meta_tokens[None].astype(x.dtype), (bsz, N_META, x.shape[-1]))
    h = jnp.concatenate([meta, x], axis=1)
    for l in range(DEPTH):
        h = mixer_sublayer(h, mix_pre_g[l], mix_post_g[l], w_in[l], b_forget[l],
                           a_dw_w[l], a_dw_b[l], a_ln_g[l], a_ln_b[l], a_pw_w[l],
                           sc_conv_w[l], head_g[l], w_out[l])
        h = ffn_sublayer(h, ffn_pre_g[l], ffn_post_g[l], ffn_w_up[l],
                         ffn_conv_w[l], ffn_w_down[l])
    return h[:, N_META:]
```

```python
import functools

import jax
import jax.numpy as jnp
from jax import lax
from jax.experimental import pallas as pl
from jax.experimental.pallas import tpu as pltpu

D_MODEL = 1024
N_META = 16
ATTN_HEADS = 8
HEAD_DIM = 64
D_ATTN = ATTN_HEADS * HEAD_DIM
D_CONV = 256
D_SC = 256
CONV_KERNEL = 31
SC_KERNEL = 3
D_FF = 2816
FFN_KERNEL = 3
RMS_EPS = 1e-6
LN_EPS = 1e-5
NEG_INF = -1e30

LANES = 128
SUBLANES = 8
BIAS_LANES = 6
SEQ_TILE = 688
ATTN_TILE = 256
CONV_HALO = 32
SHORT_HALO = 8
FF_CHUNK = 256
VMEM_LIMIT = 56 * 1024 * 1024

_OFF_Q = 0
_OFF_K = _OFF_Q + D_ATTN
_OFF_V = _OFF_K + D_ATTN
_OFF_AVAL = _OFF_V + D_ATTN
_OFF_AGATE = _OFF_AVAL + D_CONV
_OFF_SB = _OFF_AGATE + D_CONV
_OFF_SC = _OFF_SB + D_SC
_OFF_SX = _OFF_SC + D_SC
_OFF_F = _OFF_SX + D_SC
D_IN_PAD = _OFF_F + LANES

f32 = jnp.float32
bf16 = jnp.bfloat16


def _rms_norm(x, g):
    ms = jnp.mean(x * x, axis=-1, keepdims=True)
    return x * lax.rsqrt(ms + RMS_EPS) * g


def _cumsum_rows(x):
    n = x.shape[0]
    row = lax.broadcasted_iota(jnp.int32, x.shape, 0)
    shift = 1
    while shift < n:
        x = x + jnp.where(row >= shift, pltpu.roll(x, shift, axis=0), 0.0)
        shift *= 2
    return x


def _in_proj_kernel(h_ref, g_ref, w_ref, bf_ref, q_ref, k_ref, v_ref, a_ref,
                    cx_ref, sb_ref, qb_ref, kb_ref, carry_ref):
    tm = h_ref.shape[1]

    @pl.when(pl.program_id(1) == 0)
    def _():
        carry_ref[...] = jnp.zeros_like(carry_ref)

    u = _rms_norm(h_ref[0], g_ref[...]).astype(bf16)

    def seg(lo, n):
        return jnp.dot(u, w_ref[:, lo:lo + n], preferred_element_type=f32)

    q_ref[0] = (seg(_OFF_Q, D_ATTN) * (HEAD_DIM ** -0.5)).astype(bf16)
    k_ref[0] = seg(_OFF_K, D_ATTN).astype(bf16)
    v_ref[0] = seg(_OFF_V, D_ATTN).astype(bf16)
    a_ref[0] = (seg(_OFF_AVAL, D_CONV) * jax.nn.sigmoid(seg(_OFF_AGATE, D_CONV))).astype(bf16)
    sb_ref[0] = seg(_OFF_SB, D_SC).astype(bf16)
    cx_ref[0] = (seg(_OFF_SC, D_SC) * seg(_OFF_SX, D_SC)).astype(bf16)

    fl = seg(_OFF_F, LANES) + bf_ref[...]
    log_f = jnp.minimum(fl, 0.0) - jnp.log1p(jnp.exp(-jnp.abs(fl)))
    cum = _cumsum_rows(log_f) + carry_ref[0:1, :]
    carry_ref[...] = jnp.broadcast_to(cum[tm - 1:tm, :], carry_ref.shape)

    hi = cum.astype(bf16)
    r1 = cum - hi.astype(f32)
    mid = r1.astype(bf16)
    lo = (r1 - mid.astype(f32)).astype(bf16)
    lane = lax.broadcasted_iota(jnp.int32, cum.shape, 1)
    piece = lane % BIAS_LANES
    used = lane < ATTN_HEADS * BIAS_LANES
    pieces = jnp.where(piece % 3 == 0, hi, jnp.where(piece % 3 == 1, mid, lo)).astype(f32)
    qb = jnp.where(piece < 3, pieces, 1.0)
    kb = jnp.where(piece < 3, 1.0, -pieces)
    qb_ref[0] = jnp.where(used, qb, 0.0).astype(bf16)
    kb_ref[0] = jnp.where(used, kb, 0.0).astype(bf16)


def _in_proj(h, pre_g, w_in_r, bf_r):
    bsz, seq, d = h.shape
    n_tiles = seq // SEQ_TILE
    tile = lambda n: pl.BlockSpec((1, SEQ_TILE, n), lambda b, j: (b, j, 0))
    whole = lambda a: pl.BlockSpec(a.shape, lambda b, j: (0,) * a.ndim)
    out = lambda n: jax.ShapeDtypeStruct((bsz, seq, n), bf16)
    return pl.pallas_call(
        _in_proj_kernel,
        grid=(bsz, n_tiles),
        in_specs=[tile(d), whole(pre_g), whole(w_in_r), whole(bf_r)],
        out_specs=[tile(D_ATTN), tile(D_ATTN), tile(D_ATTN), tile(D_CONV),
                   tile(D_SC), tile(D_SC), tile(LANES), tile(LANES)],
        out_shape=[out(D_ATTN), out(D_ATTN), out(D_ATTN), out(D_CONV),
                   out(D_SC), out(D_SC), out(LANES), out(LANES)],
        scratch_shapes=[pltpu.VMEM((SUBLANES, LANES), f32)],
        compiler_params=pltpu.CompilerParams(
            dimension_semantics=("arbitrary", "arbitrary"),
            vmem_limit_bytes=VMEM_LIMIT),
        name="in_proj",
    )(h, pre_g, w_in_r, bf_r)


def _attn_kernel(q_ref, k_ref, v_ref, qb_ref, kb_ref, o_ref, qa_ref, ka_ref):
    seq = q_ref.shape[1]
    first = seq - (seq // ATTN_TILE - 1) * ATTN_TILE
    n_tiles = (seq - first) // ATTN_TILE + 1
    pair = pl.program_id(1)
    lane = lax.broadcasted_iota(jnp.int32, (1, LANES), 1)

    ka_ref[:, 0:LANES] = k_ref[0]
    ka_ref[:, LANES:2 * LANES] = kb_ref[0]
    for hh in range(2):
        head = 2 * pair + hh
        q_mask = (lane // HEAD_DIM == hh).astype(bf16)
        b_mask = ((lane >= BIAS_LANES * head) & (lane < BIAS_LANES * (head + 1))).astype(bf16)
        qa_ref[hh, :, 0:LANES] = q_ref[0] * q_mask
        qa_ref[hh, :, LANES:2 * LANES] = qb_ref[0] * b_mask

    def block(qi, k_start, k_size, carry, diag):
        m, l, acc = carry
        kj = ka_ref[pl.ds(k_start, k_size), :]
        s = lax.dot_general(qi, kj, (((1,), (1,)), ((), ())), preferred_element_type=f32)
        if diag:
            r = lax.broadcasted_iota(jnp.int32, s.shape, 0)
            c = lax.broadcasted_iota(jnp.int32, s.shape, 1)
            s = jnp.where(r >= c, s, NEG_INF)
        m_new = jnp.maximum(m, jnp.max(s, axis=-1, keepdims=True))
        alpha = jnp.exp(m - m_new)
        p = jnp.exp(s - m_new)
        l = alpha * l + jnp.sum(p, axis=-1, keepdims=True)
        vj = v_ref[0, pl.ds(k_start, k_size), :]
        acc = alpha * acc + jnp.dot(p.astype(bf16), vj, preferred_element_type=f32)
        return m_new, l, acc

    def init(rows):
        return (jnp.full((rows, 1), NEG_INF, f32), jnp.zeros((rows, 1), f32),
                jnp.zeros((rows, LANES), f32))

    def finish(outs):
        lo_lanes = lax.broadcasted_iota(jnp.int32, outs[0][2].shape, 1) < HEAD_DIM
        o0 = outs[0][2] * (1.0 / outs[0][1])
        o1 = outs[1][2] * (1.0 / outs[1][1])
        return jnp.where(lo_lanes, o0, o1).astype(bf16)

    outs = []
    for hh in range(2):
        outs.append(block(qa_ref[hh, 0:first, :], 0, first, init(first), True))
    o_ref[0, 0:first, :] = finish(outs)

    def q_tile(i, _):
        q_start = pl.multiple_of(first + (i - 1) * ATTN_TILE, N_META)
        outs = []
        for hh in range(2):
            qi = qa_ref[hh, pl.ds(q_start, ATTN_TILE), :]
            carry = block(qi, 0, first, init(ATTN_TILE), False)

            def k_tile(j, carry):
                k_start = pl.multiple_of(first + (j - 1) * ATTN_TILE, N_META)
                return block(qi, k_start, ATTN_TILE, carry, False)

            carry = lax.fori_loop(1, i, k_tile, carry)
            outs.append(block(qi, q_start, ATTN_TILE, carry, True))
        o_ref[0, pl.ds(q_start, ATTN_TILE), :] = finish(outs)
        return 0

    lax.fori_loop(1, n_tiles, q_tile, 0)


def _attention(q, k, v, qb, kb):
    bsz, seq, _ = q.shape
    pair_blk = pl.BlockSpec((1, seq, LANES), lambda b, p: (b, 0, p))
    bias_blk = pl.BlockSpec((1, seq, LANES), lambda b, p: (b, 0, 0))
    return pl.pallas_call(
        _attn_kernel,
        grid=(bsz, D_ATTN // LANES),
        in_specs=[pair_blk, pair_blk, pair_blk, bias_blk, bias_blk],
        out_specs=pair_blk,
        out_shape=jax.ShapeDtypeStruct((bsz, seq, D_ATTN), bf16),
        scratch_shapes=[pltpu.VMEM((2, seq, 2 * LANES), bf16),
                        pltpu.VMEM((seq, 2 * LANES), bf16)],
        compiler_params=pltpu.CompilerParams(
            dimension_semantics=("arbitrary", "arbitrary"),
            vmem_limit_bytes=VMEM_LIMIT),
        name="attention",
    )(q, k, v, qb, kb)


def _head_norm_block(y, g):
    lo_lanes = lax.broadcasted_iota(jnp.int32, y.shape, 1) < HEAD_DIM
    y2 = y * y
    s_lo = jnp.sum(jnp.where(lo_lanes, y2, 0.0), axis=-1, keepdims=True)
    s_hi = jnp.sum(jnp.where(lo_lanes, 0.0, y2), axis=-1, keepdims=True)
    r_lo = lax.rsqrt(s_lo * (1.0 / HEAD_DIM) + RMS_EPS)
    r_hi = lax.rsqrt(s_hi * (1.0 / HEAD_DIM) + RMS_EPS)
    return y * jnp.where(lo_lanes, r_lo, r_hi) * g


def _mixer_out_kernel(h_ref, ya_ref, a_ref, cx_ref, sb_ref, dw_w_ref, dw_b_ref,
                      ln_g_ref, ln_b_ref, pw_ref, scw_ref, hg_ref, wo_ref, pg_ref,
                      o_ref, a_ext, cx_ext):
    tm = h_ref.shape[1]

    @pl.when(pl.program_id(1) == 0)
    def _():
        a_ext[0:CONV_HALO, :] = jnp.zeros((CONV_HALO, D_CONV), f32)
        cx_ext[0:SHORT_HALO, :] = jnp.zeros((SHORT_HALO, D_SC), f32)

    @pl.when(pl.program_id(1) > 0)
    def _():
        a_ext[0:CONV_HALO, :] = a_ext[tm:tm + CONV_HALO, :]
        cx_ext[0:SHORT_HALO, :] = cx_ext[tm:tm + SHORT_HALO, :]

    a_ext[CONV_HALO:CONV_HALO + tm, :] = a_ref[0].astype(f32)
    cx_ext[SHORT_HALO:SHORT_HALO + tm, :] = cx_ref[0].astype(f32)

    conv = jnp.zeros((tm, D_CONV), f32) + dw_b_ref[...]
    for t in range(CONV_KERNEL):
        off = CONV_HALO - (CONV_KERNEL - 1) + t
        conv = conv + dw_w_ref[t:t + 1, :] * a_ext[off:off + tm, :]
    mu = jnp.mean(conv, axis=-1, keepdims=True)
    cen = conv - mu
    var = jnp.mean(cen * cen, axis=-1, keepdims=True)
    ln = cen * lax.rsqrt(var + LN_EPS) * ln_g_ref[...] + ln_b_ref[...]
    act = ln * jax.nn.sigmoid(ln)
    y_conv = jnp.dot(act.astype(bf16), pw_ref[...], preferred_element_type=f32)

    sc = jnp.zeros((tm, D_SC), f32)
    for t in range(SC_KERNEL):
        off = SHORT_HALO - (SC_KERNEL - 1) + t
        sc = sc + scw_ref[t:t + 1, :] * cx_ext[off:off + tm, :]
    y_sc = sb_ref[0].astype(f32) * sc

    blocks = []
    for c in range(D_MODEL // LANES):
        lo = c * LANES
        if lo < D_ATTN:
            yb = ya_ref[0, :, lo:lo + LANES].astype(f32)
        elif lo < D_ATTN + D_CONV:
            yb = y_conv[:, lo - D_ATTN:lo - D_ATTN + LANES]
        else:
            yb = y_sc[:, lo - D_ATTN - D_CONV:lo - D_ATTN - D_CONV + LANES]
        blocks.append(_head_norm_block(yb, hg_ref[:, lo:lo + LANES]).astype(bf16))
    yn = jnp.concatenate(blocks, axis=-1)
    y = jnp.dot(yn, wo_ref[...], preferred_element_type=f32)
    o_ref[0] = h_ref[0] + _rms_norm(y, pg_ref[...])


def _mixer_out(h, y_attn, a, cx, sb, dw_w, dw_b, ln_g, ln_b, pw, scw, head_g, w_out, post_g):
    bsz, seq, d = h.shape
    n_tiles = seq // SEQ_TILE
    tile = lambda n: pl.BlockSpec((1, SEQ_TILE, n), lambda b, j: (b, j, 0))
    whole = lambda x: pl.BlockSpec(x.shape, lambda b, j: (0,) * x.ndim)
    params = (dw_w, dw_b, ln_g, ln_b, pw, scw, head_g, w_out, post_g)
    return pl.pallas_call(
        _mixer_out_kernel,
        grid=(bsz, n_tiles),
        in_specs=[tile(d), tile(D_ATTN), tile(D_CONV), tile(D_SC), tile(D_SC)]
                 + [whole(x) for x in params],
        out_specs=tile(d),
        out_shape=jax.ShapeDtypeStruct(h.shape, h.dtype),
        scratch_shapes=[pltpu.VMEM((CONV_HALO + SEQ_TILE, D_CONV), f32),
                        pltpu.VMEM((SHORT_HALO + SEQ_TILE, D_SC), f32)],
        compiler_params=pltpu.CompilerParams(
            dimension_semantics=("arbitrary", "arbitrary"),
            vmem_limit_bytes=VMEM_LIMIT),
        name="mixer_out",
    )(h, y_attn, a, cx, sb, *params)


def _ffn_kernel(h_ref, pre_g_ref, wg_ref, wu_ref, cw_ref, wd_ref, post_g_ref,
                o_ref, u_ref, z_ext, halo_ref, acc_ref):
    tm = h_ref.shape[1]
    n_chunks = wg_ref.shape[0]
    fc = wg_ref.shape[2]

    @pl.when(pl.program_id(1) == 0)
    def _():
        halo_ref[...] = jnp.zeros_like(halo_ref)

    u_ref[...] = _rms_norm(h_ref[0], pre_g_ref[...]).astype(bf16)
    acc_ref[...] = jnp.zeros_like(acc_ref)

    def chunk(c, _):
        u = u_ref[...]
        z_ext[0:SHORT_HALO, :] = halo_ref[c]
        z_ext[SHORT_HALO:SHORT_HALO + tm, 0:fc] = jnp.dot(u, wg_ref[c], preferred_element_type=f32)
        z_ext[SHORT_HALO:SHORT_HALO + tm, fc:2 * fc] = jnp.dot(u, wu_ref[c], preferred_element_type=f32)
        halo_ref[c] = z_ext[tm:tm + SHORT_HALO, :]
        w = cw_ref[c]
        conv = jnp.zeros((tm, 2 * fc), f32)
        for t in range(FFN_KERNEL):
            off = SHORT_HALO - (FFN_KERNEL - 1) + t
            conv = conv + w[t:t + 1, :] * z_ext[off:off + tm, :]
        g = conv[:, 0:fc]
        act = (g * jax.nn.sigmoid(g) * conv[:, fc:2 * fc]).astype(bf16)
        acc_ref[...] += jnp.dot(act, wd_ref[c], preferred_element_type=f32)
        return 0

    lax.fori_loop(0, n_chunks, chunk, 0)
    o_ref[0] = h_ref[0] + _rms_norm(acc_ref[...], post_g_ref[...])


def _ffn(h, pre_g, wg, wu, cw, wd, post_g):
    bsz, seq, d = h.shape
    n_tiles = seq // SEQ_TILE
    n_chunks, _, fc = wg.shape
    tile = pl.BlockSpec((1, SEQ_TILE, d), lambda b, j: (b, j, 0))
    whole = lambda x: pl.BlockSpec(x.shape, lambda b, j: (0,) * x.ndim)
    params = (pre_g, wg, wu, cw, wd, post_g)
    return pl.pallas_call(
        _ffn_kernel,
        grid=(bsz, n_tiles),
        in_specs=[tile] + [whole(x) for x in params],
        out_specs=tile,
        out_shape=jax.ShapeDtypeStruct(h.shape, h.dtype),
        scratch_shapes=[pltpu.VMEM((SEQ_TILE, d), bf16),
                        pltpu.VMEM((SHORT_HALO + SEQ_TILE, 2 * fc), f32),
                        pltpu.VMEM((n_chunks, SHORT_HALO, 2 * fc), f32),
                        pltpu.VMEM((SEQ_TILE, d), f32)],
        compiler_params=pltpu.CompilerParams(
            dimension_semantics=("arbitrary", "arbitrary"),
            vmem_limit_bytes=VMEM_LIMIT),
        name="ffn",
    )(h, *params)


def _prep_w_in(w_in, b_forget):
    q, k, v, f, a_val, a_gate, sc_b, sc_c, sc_x = jnp.split(
        w_in, [512, 1024, 1536, 1544, 1800, 2056, 2312, 2568], axis=-1)
    pad = LANES - ATTN_HEADS * BIAS_LANES
    f_rep = jnp.pad(jnp.repeat(f, BIAS_LANES, axis=-1), ((0, 0), (0, pad)))
    w = jnp.concatenate([q, k, v, a_val, a_gate, sc_b, sc_c, sc_x, f_rep], axis=-1)
    bf = jnp.pad(jnp.repeat(b_forget, BIAS_LANES), (0, pad))[None, :]
    return w.astype(bf16), bf.astype(f32)


def _chunk_cols(w, n_chunks):
    k, n = w.shape
    return w.reshape(k, n_chunks, n // n_chunks).transpose(1, 0, 2)


def _prep_ffn(w_up, conv_w, w_down):
    n_chunks = D_FF // FF_CHUNK
    wg = _chunk_cols(w_up[:, :D_FF], n_chunks).astype(bf16)
    wu = _chunk_cols(w_up[:, D_FF:], n_chunks).astype(bf16)
    cg = _chunk_cols(conv_w[:, :D_FF], n_chunks)
    cu = _chunk_cols(conv_w[:, D_FF:], n_chunks)
    cw = jnp.concatenate([cg, cu], axis=-1)
    cw = jnp.pad(cw, ((0, 0), (0, SUBLANES - FFN_KERNEL), (0, 0)))
    wd = w_down.reshape(n_chunks, FF_CHUNK, D_MODEL).astype(bf16)
    return wg, wu, cw, wd


def kernel(x, meta_tokens, mix_pre_g, mix_post_g, w_in, b_forget, a_dw_w, a_dw_b, a_ln_g, a_ln_b, a_pw_w, sc_conv_w, head_g, w_out, ffn_pre_g, ffn_post_g, ffn_w_up, ffn_conv_w, ffn_w_down):
    bsz = x.shape[0]
    depth = w_in.shape[0]
    meta = jnp.broadcast_to(meta_tokens[None].astype(x.dtype), (bsz, N_META, x.shape[-1]))
    h = jnp.concatenate([meta, x], axis=1)
    row = lambda p: p[None, :]
    for l in range(depth):
        w_in_r, bf_r = _prep_w_in(w_in[l], b_forget[l])
        q, k, v, a, cx, sb, qb, kb = _in_proj(h, row(mix_pre_g[l]), w_in_r, bf_r)
        y_attn = _attention(q, k, v, qb, kb)
        h = _mixer_out(h, y_attn, a, cx, sb, a_dw_w[l], row(a_dw_b[l]), row(a_ln_g[l]),
                       row(a_ln_b[l]), a_pw_w[l].astype(bf16), sc_conv_w[l],
                       row(head_g[l]), w_out[l].astype(bf16), row(mix_post_g[l]))
        wg, wu, cw, wd = _prep_ffn(ffn_w_up[l], ffn_conv_w[l], ffn_w_down[l])
        h = _ffn(h, row(ffn_pre_g[l]), wg, wu, cw, wd, row(ffn_post_g[l]))
    return h[:, N_META:]
```

```python
import functools

import jax
import jax.numpy as jnp
from jax import lax
from jax.experimental import pallas as pl
from jax.experimental.pallas import tpu as pltpu

D_MODEL = 1024
N_META = 16
ATTN_HEADS = 8
HEAD_DIM = 64
D_ATTN = ATTN_HEADS * HEAD_DIM
D_CONV = 256
D_SC = 256
CONV_KERNEL = 31
SC_KERNEL = 3
D_FF = 2816
FFN_KERNEL = 3
RMS_EPS = 1e-6
LN_EPS = 1e-5
NEG_INF = -1e30

LANES = 128
SUBLANES = 8
BIAS_LANES = 6
SEQ_TILE = 688
ATTN_TILE = 256
CONV_HALO = 32
SHORT_HALO = 8
CONV_ROWS = 16
FF_CHUNK = 256
VMEM_LIMIT = 56 * 1024 * 1024

_OFF_Q = 0
_OFF_K = _OFF_Q + D_ATTN
_OFF_V = _OFF_K + D_ATTN
_OFF_AVAL = _OFF_V + D_ATTN
_OFF_AGATE = _OFF_AVAL + D_CONV
_OFF_SB = _OFF_AGATE + D_CONV
_OFF_SC = _OFF_SB + D_SC
_OFF_SX = _OFF_SC + D_SC
_OFF_F = _OFF_SX + D_SC
D_IN_PAD = _OFF_F + LANES

f32 = jnp.float32
bf16 = jnp.bfloat16


def _rms_norm(x, g):
    ms = jnp.mean(x * x, axis=-1, keepdims=True)
    return x * lax.rsqrt(ms + RMS_EPS) * g


def _cumsum_rows(x):
    n = x.shape[0]
    row = lax.broadcasted_iota(jnp.int32, x.shape, 0)
    shift = 1
    while shift < n:
        x = x + jnp.where(row >= shift, pltpu.roll(x, shift, axis=0), 0.0)
        shift *= 2
    return x


def _in_proj_kernel(h_ref, g_ref, w_ref, bf_ref, q_ref, k_ref, v_ref, a_ref,
                    cx_ref, sb_ref, qb_ref, kb_ref, carry_ref):
    tm = h_ref.shape[1]

    @pl.when(pl.program_id(1) == 0)
    def _():
        carry_ref[...] = jnp.zeros_like(carry_ref)

    u = _rms_norm(h_ref[0], g_ref[...]).astype(bf16)

    def seg(lo, n):
        return jnp.dot(u, w_ref[:, lo:lo + n], preferred_element_type=f32)

    q_ref[0] = (seg(_OFF_Q, D_ATTN) * (HEAD_DIM ** -0.5)).astype(bf16)
    k_ref[0] = seg(_OFF_K, D_ATTN).astype(bf16)
    v_ref[0] = seg(_OFF_V, D_ATTN).astype(bf16)
    a_ref[0] = (seg(_OFF_AVAL, D_CONV) * jax.nn.sigmoid(seg(_OFF_AGATE, D_CONV))).astype(bf16)
    sb_ref[0] = seg(_OFF_SB, D_SC).astype(bf16)
    cx_ref[0] = (seg(_OFF_SC, D_SC) * seg(_OFF_SX, D_SC)).astype(bf16)

    fl = seg(_OFF_F, LANES) + bf_ref[...]
    log_f = jnp.minimum(fl, 0.0) - jnp.log1p(jnp.exp(-jnp.abs(fl)))
    cum = _cumsum_rows(log_f) + carry_ref[0:1, :]
    carry_ref[...] = jnp.broadcast_to(cum[tm - 1:tm, :], carry_ref.shape)

    hi = cum.astype(bf16)
    r1 = cum - hi.astype(f32)
    mid = r1.astype(bf16)
    lo = (r1 - mid.astype(f32)).astype(bf16)
    lane = lax.broadcasted_iota(jnp.int32, cum.shape, 1)
    piece = lane % BIAS_LANES
    used = lane < ATTN_HEADS * BIAS_LANES
    pieces = jnp.where(piece % 3 == 0, hi, jnp.where(piece % 3 == 1, mid, lo)).astype(f32)
    qb = jnp.where(piece < 3, pieces, 1.0)
    kb = jnp.where(piece < 3, 1.0, -pieces)
    qb_ref[0] = jnp.where(used, qb, 0.0).astype(bf16)
    kb_ref[0] = jnp.where(used, kb, 0.0).astype(bf16)


def _in_proj(h, pre_g, w_in_r, bf_r):
    bsz, seq, d = h.shape
    n_tiles = seq // SEQ_TILE
    tile = lambda n: pl.BlockSpec((1, SEQ_TILE, n), lambda b, j: (b, j, 0))
    whole = lambda a: pl.BlockSpec(a.shape, lambda b, j: (0,) * a.ndim)
    out = lambda n: jax.ShapeDtypeStruct((bsz, seq, n), bf16)
    return pl.pallas_call(
        _in_proj_kernel,
        grid=(bsz, n_tiles),
        in_specs=[tile(d), whole(pre_g), whole(w_in_r), whole(bf_r)],
        out_specs=[tile(D_ATTN), tile(D_ATTN), tile(D_ATTN), tile(D_CONV),
                   tile(D_SC), tile(D_SC), tile(LANES), tile(LANES)],
        out_shape=[out(D_ATTN), out(D_ATTN), out(D_ATTN), out(D_CONV),
                   out(D_SC), out(D_SC), out(LANES), out(LANES)],
        scratch_shapes=[pltpu.VMEM((SUBLANES, LANES), f32)],
        compiler_params=pltpu.CompilerParams(
            dimension_semantics=("arbitrary", "arbitrary"),
            vmem_limit_bytes=VMEM_LIMIT),
        name="in_proj",
    )(h, pre_g, w_in_r, bf_r)


def _attn_kernel(q_ref, k_ref, v_ref, qb_ref, kb_ref, o_ref, vt_ref, vtm_ref):
    seq = q_ref.shape[1]
    n_tiles = (seq - N_META) // ATTN_TILE
    pair = pl.program_id(1)
    lane = lax.broadcasted_iota(jnp.int32, (1, LANES), 1)
    nt_dims = (((1,), (1,)), ((), ()))

    q_masks, b_masks = [], []
    for hh in range(2):
        head = 2 * pair + hh
        q_masks.append((lane // HEAD_DIM == hh).astype(bf16))
        b_masks.append(((lane >= BIAS_LANES * head) & (lane < BIAS_LANES * (head + 1))).astype(bf16))

    def queries(start, size, hh):
        return jnp.concatenate([q_ref[0, pl.ds(start, size), :] * q_masks[hh],
                                qb_ref[0, pl.ds(start, size), :] * b_masks[hh]], axis=1)

    def keys(start, size):
        return jnp.concatenate([k_ref[0, pl.ds(start, size), :],
                                kb_ref[0, pl.ds(start, size), :]], axis=1)

    v_meta = jnp.concatenate([v_ref[0, 0:N_META, :].astype(f32),
                              jnp.zeros((LANES - N_META, LANES), f32)], axis=0)
    vtm_ref[...] = v_meta.T.astype(bf16)
    for j in range(n_tiles):
        lo = N_META + j * ATTN_TILE
        vt_ref[j] = v_ref[0, lo:lo + ATTN_TILE, :].astype(f32).T.astype(bf16)

    lo_lanes = lax.broadcasted_iota(jnp.int32, (N_META, LANES), 1) < HEAD_DIM
    k_meta = keys(0, N_META)
    o_meta = []
    for hh in range(2):
        s = lax.dot_general(queries(0, N_META, hh), k_meta, nt_dims, preferred_element_type=f32)
        r = lax.broadcasted_iota(jnp.int32, s.shape, 0)
        c = lax.broadcasted_iota(jnp.int32, s.shape, 1)
        s = jnp.where(r >= c, s, NEG_INF)
        p = jnp.exp(s - jnp.max(s, axis=-1, keepdims=True))
        o = jnp.dot(p.astype(bf16), v_ref[0, 0:N_META, :], preferred_element_type=f32)
        o_meta.append(o * (1.0 / jnp.sum(p, axis=-1, keepdims=True)))
    o_ref[0, 0:N_META, :] = jnp.where(lo_lanes, o_meta[0], o_meta[1]).astype(bf16)

    def q_tile(i, _):
        q_start = pl.multiple_of(N_META + i * ATTN_TILE, N_META)
        qa = [queries(q_start, ATTN_TILE, hh) for hh in range(2)]

        def update(carry, st, vt, hh):
            m, l, acc = carry
            m_new = jnp.maximum(m, jnp.max(st, axis=0, keepdims=True))
            alpha = jnp.exp(m - m_new)
            pt = jnp.exp(st - m_new)
            l = alpha * l + jnp.sum(pt, axis=0, keepdims=True)
            pt = pt.astype(bf16)
            if pt.shape[0] < vt.shape[1]:
                pt = jnp.concatenate(
                    [pt, jnp.zeros((vt.shape[1] - pt.shape[0], pt.shape[1]), bf16)], axis=0)
            vt_h = vt[hh * HEAD_DIM:(hh + 1) * HEAD_DIM, :]
            acc = alpha * acc + jnp.dot(vt_h, pt, preferred_element_type=f32)
            return m_new, l, acc

        carries = []
        for hh in range(2):
            st = lax.dot_general(k_meta, qa[hh], nt_dims, preferred_element_type=f32)
            init = (jnp.full((1, ATTN_TILE), NEG_INF, f32), jnp.zeros((1, ATTN_TILE), f32),
                    jnp.zeros((HEAD_DIM, ATTN_TILE), f32))
            carries.append(update(init, st, vtm_ref[...], hh))

        def scores(j):
            kj = keys(pl.multiple_of(N_META + j * ATTN_TILE, N_META), ATTN_TILE)
            return tuple(lax.dot_general(kj, qa[hh], nt_dims, preferred_element_type=f32)
                         for hh in range(2))

        def k_tile(j, state):
            carries, sts = state
            sts_next = scores(j + 1)
            vt = vt_ref[j]
            return (tuple(update(carries[hh], sts[hh], vt, hh) for hh in range(2)), sts_next)

        carries, sts = lax.fori_loop(0, i, k_tile, (tuple(carries), scores(0)))

        vt = vt_ref[i]
        rows = []
        for hh in range(2):
            kk = lax.broadcasted_iota(jnp.int32, sts[hh].shape, 0)
            qq = lax.broadcasted_iota(jnp.int32, sts[hh].shape, 1)
            st = jnp.where(qq >= kk, sts[hh], NEG_INF)
            _, l, acc = update(carries[hh], st, vt, hh)
            rows.append(acc * (1.0 / l))
        o_t = jnp.concatenate(rows, axis=0)
        o_ref[0, pl.ds(q_start, ATTN_TILE), :] = o_t.T.astype(bf16)
        return 0

    lax.fori_loop(0, n_tiles, q_tile, 0)


def _attention(q, k, v, qb, kb):
    bsz, seq, _ = q.shape
    n_tiles = (seq - N_META) // ATTN_TILE
    pair_blk = pl.BlockSpec((1, seq, LANES), lambda b, p: (b, 0, p))
    bias_blk = pl.BlockSpec((1, seq, LANES), lambda b, p: (b, 0, 0))
    return pl.pallas_call(
        _attn_kernel,
        grid=(bsz, D_ATTN // LANES),
        in_specs=[pair_blk, pair_blk, pair_blk, bias_blk, bias_blk],
        out_specs=pair_blk,
        out_shape=jax.ShapeDtypeStruct((bsz, seq, D_ATTN), bf16),
        scratch_shapes=[pltpu.VMEM((n_tiles, LANES, ATTN_TILE), bf16),
                        pltpu.VMEM((LANES, LANES), bf16)],
        compiler_params=pltpu.CompilerParams(
            dimension_semantics=("arbitrary", "arbitrary"),
            vmem_limit_bytes=VMEM_LIMIT),
        name="attention",
    )(q, k, v, qb, kb)


def _head_norm_block(y, g):
    lo_lanes = lax.broadcasted_iota(jnp.int32, y.shape, 1) < HEAD_DIM
    y2 = y * y
    s_lo = jnp.sum(jnp.where(lo_lanes, y2, 0.0), axis=-1, keepdims=True)
    s_hi = jnp.sum(jnp.where(lo_lanes, 0.0, y2), axis=-1, keepdims=True)
    r_lo = lax.rsqrt(s_lo * (1.0 / HEAD_DIM) + RMS_EPS)
    r_hi = lax.rsqrt(s_hi * (1.0 / HEAD_DIM) + RMS_EPS)
    return y * jnp.where(lo_lanes, r_lo, r_hi) * g


def _mixer_out_kernel(h_ref, ya_ref, a_ref, cx_ref, sb_ref, dw_w_ref, dw_b_ref,
                      ln_g_ref, ln_b_ref, pw_ref, scw_ref, hg_ref, wo_ref, pg_ref,
                      o_ref, a_ext, cx_ext, a_sh, conv_ref, w_bc):
    tm = h_ref.shape[1]

    @pl.when(pl.program_id(1) == 0)
    def _():
        a_ext[0:CONV_HALO, :] = jnp.zeros((CONV_HALO, D_CONV), f32)
        cx_ext[0:SHORT_HALO, :] = jnp.zeros((SHORT_HALO, D_SC), f32)

    @pl.when(pl.program_id(1) > 0)
    def _():
        a_ext[0:CONV_HALO, :] = a_ext[tm:tm + CONV_HALO, :]
        cx_ext[0:SHORT_HALO, :] = cx_ext[tm:tm + SHORT_HALO, :]

    a_ext[CONV_HALO:CONV_HALO + tm, :] = a_ref[0].astype(f32)
    cx_ext[SHORT_HALO:SHORT_HALO + tm, :] = cx_ref[0].astype(f32)

    span = tm + CONV_HALO - SUBLANES
    for r in range(1, SUBLANES):
        a_sh[r - 1] = a_ext[r:r + span, :]
    for t in range(CONV_KERNEL):
        w_bc[t] = jnp.broadcast_to(dw_w_ref[t:t + 1, :], (SUBLANES, D_CONV))

    def conv_rows(rb, _):
        base = pl.multiple_of(rb * CONV_ROWS, CONV_ROWS)
        acc = jnp.broadcast_to(dw_b_ref[...], (CONV_ROWS, D_CONV))
        first_off = CONV_HALO - (CONV_KERNEL - 1)
        for r in range(SUBLANES):
            ms = [off // SUBLANES for off in range(first_off, CONV_HALO + 1)
                  if off % SUBLANES == r]
            rows = pl.ds(pl.multiple_of(base + ms[0] * SUBLANES, SUBLANES),
                         (ms[-1] - ms[0]) * SUBLANES + CONV_ROWS)
            window = a_ext[rows, :] if r == 0 else a_sh[r - 1, rows, :]
            for m in ms:
                t = m * SUBLANES + r - first_off
                lo = (m - ms[0]) * SUBLANES
                w = jnp.concatenate([w_bc[t]] * (CONV_ROWS // SUBLANES), axis=0)
                acc = acc + w * window[lo:lo + CONV_ROWS, :]
        conv_ref[pl.ds(base, CONV_ROWS), :] = acc
        return 0

    lax.fori_loop(0, tm // CONV_ROWS, conv_rows, 0)
    conv = conv_ref[...]
    mu = jnp.mean(conv, axis=-1, keepdims=True)
    cen = conv - mu
    var = jnp.mean(cen * cen, axis=-1, keepdims=True)
    ln = cen * lax.rsqrt(var + LN_EPS) * ln_g_ref[...] + ln_b_ref[...]
    act = ln * jax.nn.sigmoid(ln)
    y_conv = jnp.dot(act.astype(bf16), pw_ref[...], preferred_element_type=f32)

    sc = jnp.zeros((tm, D_SC), f32)
    for t in range(SC_KERNEL):
        off = SHORT_HALO - (SC_KERNEL - 1) + t
        sc = sc + scw_ref[t:t + 1, :] * cx_ext[off:off + tm, :]
    y_sc = sb_ref[0].astype(f32) * sc

    blocks = []
    for c in range(D_MODEL // LANES):
        lo = c * LANES
        if lo < D_ATTN:
            yb = ya_ref[0, :, lo:lo + LANES].astype(f32)
        elif lo < D_ATTN + D_CONV:
            yb = y_conv[:, lo - D_ATTN:lo - D_ATTN + LANES]
        else:
            yb = y_sc[:, lo - D_ATTN - D_CONV:lo - D_ATTN - D_CONV + LANES]
        blocks.append(_head_norm_block(yb, hg_ref[:, lo:lo + LANES]).astype(bf16))
    yn = jnp.concatenate(blocks, axis=-1)
    y = jnp.dot(yn, wo_ref[...], preferred_element_type=f32)
    o_ref[0] = h_ref[0] + _rms_norm(y, pg_ref[...])


def _mixer_out(h, y_attn, a, cx, sb, dw_w, dw_b, ln_g, ln_b, pw, scw, head_g, w_out, post_g):
    bsz, seq, d = h.shape
    n_tiles = seq // SEQ_TILE
    tile = lambda n: pl.BlockSpec((1, SEQ_TILE, n), lambda b, j: (b, j, 0))
    whole = lambda x: pl.BlockSpec(x.shape, lambda b, j: (0,) * x.ndim)
    params = (dw_w, dw_b, ln_g, ln_b, pw, scw, head_g, w_out, post_g)
    return pl.pallas_call(
        _mixer_out_kernel,
        grid=(bsz, n_tiles),
        in_specs=[tile(d), tile(D_ATTN), tile(D_CONV), tile(D_SC), tile(D_SC)]
                 + [whole(x) for x in params],
        out_specs=tile(d),
        out_shape=jax.ShapeDtypeStruct(h.shape, h.dtype),
        scratch_shapes=[pltpu.VMEM((CONV_HALO + SEQ_TILE, D_CONV), f32),
                        pltpu.VMEM((SHORT_HALO + SEQ_TILE, D_SC), f32),
                        pltpu.VMEM((SUBLANES - 1, SEQ_TILE + CONV_HALO - SUBLANES, D_CONV), f32),
                        pltpu.VMEM((SEQ_TILE, D_CONV), f32),
                        pltpu.VMEM((CONV_KERNEL, SUBLANES, D_CONV), f32)],
        compiler_params=pltpu.CompilerParams(
            dimension_semantics=("arbitrary", "arbitrary"),
            vmem_limit_bytes=VMEM_LIMIT),
        name="mixer_out",
    )(h, y_attn, a, cx, sb, *params)


def _ffn_kernel(h_ref, pre_g_ref, wg_ref, wu_ref, cw_ref, wd_ref, post_g_ref,
                o_ref, u_ref, z_ext, halo_ref, acc_ref):
    tm = h_ref.shape[1]
    n_chunks = wg_ref.shape[0]
    fc = wg_ref.shape[2]

    @pl.when(pl.program_id(1) == 0)
    def _():
        halo_ref[...] = jnp.zeros_like(halo_ref)

    u_ref[...] = _rms_norm(h_ref[0], pre_g_ref[...]).astype(bf16)
    acc_ref[...] = jnp.zeros_like(acc_ref)

    def up(c, slot):
        u = u_ref[...]
        z_ext[slot, 0:SHORT_HALO, :] = halo_ref[c]
        z_ext[slot, SHORT_HALO:SHORT_HALO + tm, 0:fc] = jnp.dot(
            u, wg_ref[c], preferred_element_type=f32)
        z_ext[slot, SHORT_HALO:SHORT_HALO + tm, fc:2 * fc] = jnp.dot(
            u, wu_ref[c], preferred_element_type=f32)
        halo_ref[c] = z_ext[slot, tm:tm + SHORT_HALO, :]

    def down(c, slot):
        w = cw_ref[c]
        conv = jnp.zeros((tm, 2 * fc), f32)
        for t in range(FFN_KERNEL):
            off = SHORT_HALO - (FFN_KERNEL - 1) + t
            conv = conv + w[t:t + 1, :] * z_ext[slot, off:off + tm, :]
        g = conv[:, 0:fc]
        act = (g * jax.nn.sigmoid(g) * conv[:, fc:2 * fc]).astype(bf16)
        acc_ref[...] += jnp.dot(act, wd_ref[c], preferred_element_type=f32)

    assert n_chunks % 2 == 1
    up(0, 0)

    def chunk_pair(k, _):
        c = 2 * k
        up(c + 1, 1)
        down(c, 0)
        up(c + 2, 0)
        down(c + 1, 1)
        return 0

    lax.fori_loop(0, n_chunks // 2, chunk_pair, 0)
    down(n_chunks - 1, 0)
    o_ref[0] = h_ref[0] + _rms_norm(acc_ref[...], post_g_ref[...])


def _ffn(h, pre_g, wg, wu, cw, wd, post_g):
    bsz, seq, d = h.shape
    n_tiles = seq // SEQ_TILE
    n_chunks, _, fc = wg.shape
    tile = pl.BlockSpec((1, SEQ_TILE, d), lambda b, j: (b, j, 0))
    whole = lambda x: pl.BlockSpec(x.shape, lambda b, j: (0,) * x.ndim)
    params = (pre_g, wg, wu, cw, wd, post_g)
    return pl.pallas_call(
        _ffn_kernel,
        grid=(bsz, n_tiles),
        in_specs=[tile] + [whole(x) for x in params],
        out_specs=tile,
        out_shape=jax.ShapeDtypeStruct(h.shape, h.dtype),
        scratch_shapes=[pltpu.VMEM((SEQ_TILE, d), bf16),
                        pltpu.VMEM((2, SHORT_HALO + SEQ_TILE, 2 * fc), f32),
                        pltpu.VMEM((n_chunks, SHORT_HALO, 2 * fc), f32),
                        pltpu.VMEM((SEQ_TILE, d), f32)],
        compiler_params=pltpu.CompilerParams(
            dimension_semantics=("arbitrary", "arbitrary"),
            vmem_limit_bytes=VMEM_LIMIT),
        name="ffn",
    )(h, *params)


def _prep_w_in(w_in, b_forget):
    q, k, v, f, a_val, a_gate, sc_b, sc_c, sc_x = jnp.split(
        w_in, [512, 1024, 1536, 1544, 1800, 2056, 2312, 2568], axis=-1)
    pad = LANES - ATTN_HEADS * BIAS_LANES
    f_rep = jnp.pad(jnp.repeat(f, BIAS_LANES, axis=-1), ((0, 0), (0, pad)))
    w = jnp.concatenate([q, k, v, a_val, a_gate, sc_b, sc_c, sc_x, f_rep], axis=-1)
    bf = jnp.pad(jnp.repeat(b_forget, BIAS_LANES), (0, pad))[None, :]
    return w.astype(bf16), bf.astype(f32)


def _chunk_cols(w, n_chunks):
    k, n = w.shape
    return w.reshape(k, n_chunks, n // n_chunks).transpose(1, 0, 2)


def _prep_ffn(w_up, conv_w, w_down):
    n_chunks = D_FF // FF_CHUNK
    wg = _chunk_cols(w_up[:, :D_FF], n_chunks).astype(bf16)
    wu = _chunk_cols(w_up[:, D_FF:], n_chunks).astype(bf16)
    cg = _chunk_cols(conv_w[:, :D_FF], n_chunks)
    cu = _chunk_cols(conv_w[:, D_FF:], n_chunks)
    cw = jnp.concatenate([cg, cu], axis=-1)
    cw = jnp.pad(cw, ((0, 0), (0, SUBLANES - FFN_KERNEL), (0, 0)))
    wd = w_down.reshape(n_chunks, FF_CHUNK, D_MODEL).astype(bf16)
    return wg, wu, cw, wd


def kernel(x, meta_tokens, mix_pre_g, mix_post_g, w_in, b_forget, a_dw_w, a_dw_b, a_ln_g, a_ln_b, a_pw_w, sc_conv_w, head_g, w_out, ffn_pre_g, ffn_post_g, ffn_w_up, ffn_conv_w, ffn_w_down):
    bsz = x.shape[0]
    depth = w_in.shape[0]
    meta = jnp.broadcast_to(meta_tokens[None].astype(x.dtype), (bsz, N_META, x.shape[-1]))
    h = jnp.concatenate([meta, x], axis=1)
    row = lambda p: p[None, :]
    for l in range(depth):
        w_in_r, bf_r = _prep_w_in(w_in[l], b_forget[l])
        q, k, v, a, cx, sb, qb, kb = _in_proj(h, row(mix_pre_g[l]), w_in_r, bf_r)
        y_attn = _attention(q, k, v, qb, kb)
        h = _mixer_out(h, y_attn, a, cx, sb, a_dw_w[l], row(a_dw_b[l]), row(a_ln_g[l]),
                       row(a_ln_b[l]), a_pw_w[l].astype(bf16), sc_conv_w[l],
                       row(head_g[l]), w_out[l].astype(bf16), row(mix_post_g[l]))
        wg, wu, cw, wd = _prep_ffn(ffn_w_up[l], ffn_conv_w[l], ffn_w_down[l])
        h = _ffn(h, row(ffn_pre_g[l]), wg, wu, cw, wd, row(ffn_post_g[l]))
    return h[:, N_META:]
```

```python
import functools

import jax
import jax.numpy as jnp
from jax import lax
from jax.experimental import pallas as pl
from jax.experimental.pallas import tpu as pltpu

D_MODEL = 1024
N_META = 16
ATTN_HEADS = 8
HEAD_DIM = 64
D_ATTN = ATTN_HEADS * HEAD_DIM
D_CONV = 256
D_SC = 256
CONV_KERNEL = 31
SC_KERNEL = 3
D_FF = 2816
FFN_KERNEL = 3
RMS_EPS = 1e-6
LN_EPS = 1e-5
NEG_INF = -1e30
LOG2E = 1.4426950408889634

LANES = 128
SUBLANES = 8
BIAS_LANES = 6
SEQ_TILE = 688
ATTN_TILE = 256
ATTN_QUERIES = 512
CONV_HALO = 32
SHORT_HALO = 8
CONV_ROWS = 16
FF_CHUNK = 256
VMEM_LIMIT = 56 * 1024 * 1024

_OFF_Q = 0
_OFF_K = _OFF_Q + D_ATTN
_OFF_V = _OFF_K + D_ATTN
_OFF_AVAL = _OFF_V + D_ATTN
_OFF_AGATE = _OFF_AVAL + D_CONV
_OFF_SB = _OFF_AGATE + D_CONV
_OFF_SC = _OFF_SB + D_SC
_OFF_SX = _OFF_SC + D_SC
_OFF_F = _OFF_SX + D_SC
D_IN_PAD = _OFF_F + LANES

f32 = jnp.float32
bf16 = jnp.bfloat16


def _rms_norm(x, g):
    ms = jnp.mean(x * x, axis=-1, keepdims=True)
    return x * lax.rsqrt(ms + RMS_EPS) * g


def _cumsum_rows(x):
    n = x.shape[0]
    row = lax.broadcasted_iota(jnp.int32, x.shape, 0)
    shift = 1
    while shift < n:
        x = x + jnp.where(row >= shift, pltpu.roll(x, shift, axis=0), 0.0)
        shift *= 2
    return x


def _in_proj_kernel(h_ref, g_ref, w_ref, bf_ref, q_ref, k_ref, v_ref, a_ref,
                    cx_ref, sb_ref, qb_ref, kb_ref, carry_ref):
    tm = h_ref.shape[1]

    @pl.when(pl.program_id(1) == 0)
    def _():
        carry_ref[...] = jnp.zeros_like(carry_ref)

    u = _rms_norm(h_ref[0], g_ref[...]).astype(bf16)

    def seg(lo, n):
        return jnp.dot(u, w_ref[:, lo:lo + n], preferred_element_type=f32)

    q_ref[0] = (seg(_OFF_Q, D_ATTN) * (HEAD_DIM ** -0.5 * LOG2E)).astype(bf16)
    k_ref[0] = seg(_OFF_K, D_ATTN).astype(bf16)
    v_ref[0] = seg(_OFF_V, D_ATTN).astype(bf16)
    a_ref[0] = (seg(_OFF_AVAL, D_CONV) * jax.nn.sigmoid(seg(_OFF_AGATE, D_CONV))).astype(bf16)
    sb_ref[0] = seg(_OFF_SB, D_SC).astype(bf16)
    cx_ref[0] = (seg(_OFF_SC, D_SC) * seg(_OFF_SX, D_SC)).astype(bf16)

    fl = seg(_OFF_F, LANES) + bf_ref[...]
    log_f = jnp.minimum(fl, 0.0) - jnp.log1p(jnp.exp(-jnp.abs(fl)))
    cum = _cumsum_rows(log_f) + carry_ref[0:1, :]
    carry_ref[...] = jnp.broadcast_to(cum[tm - 1:tm, :], carry_ref.shape)

    cum2 = cum * LOG2E
    hi = cum2.astype(bf16)
    r1 = cum2 - hi.astype(f32)
    mid = r1.astype(bf16)
    lo = (r1 - mid.astype(f32)).astype(bf16)
    lane = lax.broadcasted_iota(jnp.int32, cum.shape, 1)
    piece = lane % BIAS_LANES
    used = lane < ATTN_HEADS * BIAS_LANES
    pieces = jnp.where(piece % 3 == 0, hi, jnp.where(piece % 3 == 1, mid, lo)).astype(f32)
    qb = jnp.where(piece < 3, pieces, 1.0)
    kb = jnp.where(piece < 3, 1.0, -pieces)
    qb_ref[0] = jnp.where(used, qb, 0.0).astype(bf16)
    kb_ref[0] = jnp.where(used, kb, 0.0).astype(bf16)


def _in_proj(h, pre_g, w_in_r, bf_r):
    bsz, seq, d = h.shape
    n_tiles = seq // SEQ_TILE
    tile = lambda n: pl.BlockSpec((1, SEQ_TILE, n), lambda b, j: (b, j, 0))
    whole = lambda a: pl.BlockSpec(a.shape, lambda b, j: (0,) * a.ndim)
    out = lambda n: jax.ShapeDtypeStruct((bsz, seq, n), bf16)
    return pl.pallas_call(
        _in_proj_kernel,
        grid=(bsz, n_tiles),
        in_specs=[tile(d), whole(pre_g), whole(w_in_r), whole(bf_r)],
        out_specs=[tile(D_ATTN), tile(D_ATTN), tile(D_ATTN), tile(D_CONV),
                   tile(D_SC), tile(D_SC), tile(LANES), tile(LANES)],
        out_shape=[out(D_ATTN), out(D_ATTN), out(D_ATTN), out(D_CONV),
                   out(D_SC), out(D_SC), out(LANES), out(LANES)],
        scratch_shapes=[pltpu.VMEM((SUBLANES, LANES), f32)],
        compiler_params=pltpu.CompilerParams(
            dimension_semantics=("arbitrary", "arbitrary"),
            vmem_limit_bytes=VMEM_LIMIT),
        name="in_proj",
    )(h, pre_g, w_in_r, bf_r)


def _attn_kernel(q_ref, k_ref, v_ref, qb_ref, kb_ref, o_ref, vt_ref, vtm_ref):
    seq = q_ref.shape[1]
    n_tiles = (seq - N_META) // ATTN_TILE
    pair = pl.program_id(1)
    lane = lax.broadcasted_iota(jnp.int32, (1, LANES), 1)
    nt_dims = (((1,), (1,)), ((), ()))

    q_masks, b_masks = [], []
    for hh in range(2):
        head = 2 * pair + hh
        q_masks.append((lane // HEAD_DIM == hh).astype(bf16))
        b_masks.append(((lane >= BIAS_LANES * head) & (lane < BIAS_LANES * (head + 1))).astype(bf16))

    def queries(start, size, hh):
        return jnp.concatenate([q_ref[0, pl.ds(start, size), :] * q_masks[hh],
                                qb_ref[0, pl.ds(start, size), :] * b_masks[hh]], axis=1)

    def keys(start, size):
        return jnp.concatenate([k_ref[0, pl.ds(start, size), :],
                                kb_ref[0, pl.ds(start, size), :]], axis=1)

    v_meta = jnp.concatenate([v_ref[0, 0:N_META, :].astype(f32),
                              jnp.zeros((LANES - N_META, LANES), f32)], axis=0)
    vtm_ref[...] = v_meta.T.astype(bf16)
    for j in range(n_tiles):
        lo = N_META + j * ATTN_TILE
        vt_ref[j] = v_ref[0, lo:lo + ATTN_TILE, :].astype(f32).T.astype(bf16)

    lo_lanes = lax.broadcasted_iota(jnp.int32, (N_META, LANES), 1) < HEAD_DIM
    k_meta = keys(0, N_META)
    o_meta = []
    for hh in range(2):
        s = lax.dot_general(queries(0, N_META, hh), k_meta, nt_dims, preferred_element_type=f32)
        r = lax.broadcasted_iota(jnp.int32, s.shape, 0)
        c = lax.broadcasted_iota(jnp.int32, s.shape, 1)
        s = jnp.where(r >= c, s, NEG_INF)
        p = jnp.exp2(s - jnp.max(s, axis=-1, keepdims=True))
        o = jnp.dot(p.astype(bf16), v_ref[0, 0:N_META, :], preferred_element_type=f32)
        o_meta.append(o * (1.0 / jnp.sum(p, axis=-1, keepdims=True)))
    o_ref[0, 0:N_META, :] = jnp.where(lo_lanes, o_meta[0], o_meta[1]).astype(bf16)

    first = N_META + ATTN_TILE
    n_q = (seq - N_META) // ATTN_QUERIES
    k_per_q = ATTN_QUERIES // ATTN_TILE

    def queries_t(i):
        start = pl.multiple_of(N_META + i * ATTN_QUERIES, N_META)
        return tuple(queries(start, ATTN_QUERIES, hh).astype(f32).T.astype(bf16)
                     for hh in range(2))

    def scores(start, size, qt):
        kj = keys(start, size)
        return tuple(jnp.dot(kj, qt[hh], preferred_element_type=f32) for hh in range(2))

    def first_block(st, hh, mask):
        if mask is not None:
            st = jnp.where(mask, st, NEG_INF)
        m = jnp.max(st, axis=0, keepdims=True)
        pt = jnp.exp2(st - m)
        l = jnp.sum(pt, axis=0, keepdims=True)
        pt = pt.astype(bf16)
        pt_meta = jnp.concatenate(
            [pt[0:N_META], jnp.zeros((LANES - N_META, ATTN_QUERIES), bf16)], axis=0)
        head = slice(hh * HEAD_DIM, (hh + 1) * HEAD_DIM)
        acc = (jnp.dot(vt_ref[0, head, :], pt[N_META:first], preferred_element_type=f32)
               + jnp.dot(vtm_ref[head, :], pt_meta, preferred_element_type=f32))
        return m, l, acc

    def update(carry, st, vt, hh):
        m, l, acc = carry
        m_new = jnp.maximum(m, jnp.max(st, axis=0, keepdims=True))
        alpha = jnp.exp2(m - m_new)
        pt = jnp.exp2(st - m_new)
        l = alpha * l + jnp.sum(pt, axis=0, keepdims=True)
        vt_h = vt[hh * HEAD_DIM:(hh + 1) * HEAD_DIM, :]
        acc = alpha * acc + jnp.dot(vt_h, pt.astype(bf16), preferred_element_type=f32)
        return m_new, l, acc

    def store(i, carries):
        o_t = jnp.concatenate([acc * (1.0 / l) for _, l, acc in carries], axis=0)
        start = pl.multiple_of(N_META + i * ATTN_QUERIES, N_META)
        o_ref[0, pl.ds(start, ATTN_QUERIES), :] = o_t.T.astype(bf16)

    def tile_scores(j, qt):
        return scores(pl.multiple_of(N_META + j * ATTN_TILE, N_META), ATTN_TILE, qt)

    def diagonal(i, carries, sts, qt, first_t):
        for t in range(first_t, k_per_q):
            if sts is None or t > first_t:
                sts = tile_scores(i * k_per_q + t, qt)
            kk = lax.broadcasted_iota(jnp.int32, sts[0].shape, 0) + t * ATTN_TILE
            qq = lax.broadcasted_iota(jnp.int32, sts[0].shape, 1)
            vt = vt_ref[i * k_per_q + t]
            carries = [update(carries[hh], jnp.where(qq >= kk, sts[hh], NEG_INF), vt, hh)
                       for hh in range(2)]
        return carries

    qt = queries_t(0)
    sts = scores(0, first, qt)
    kk = lax.broadcasted_iota(jnp.int32, sts[0].shape, 0)
    qq = lax.broadcasted_iota(jnp.int32, sts[0].shape, 1)
    carries = [first_block(sts[hh], hh, qq >= kk - N_META) for hh in range(2)]
    store(0, diagonal(0, carries, None, qt, 1))

    def q_tile(i, state):
        sts_first, qt = state
        sts = tile_scores(1, qt)
        carries = tuple(first_block(sts_first[hh], hh, None) for hh in range(2))

        def k_tile(j, state):
            carries, sts = state
            sts_next = tile_scores(j + 1, qt)
            vt = vt_ref[j]
            return (tuple(update(carries[hh], sts[hh], vt, hh) for hh in range(2)), sts_next)

        carries, sts = lax.fori_loop(1, i * k_per_q, k_tile, (carries, sts))

        qt_next = queries_t(jnp.minimum(i + 1, n_q - 1))
        sts_first_next = scores(0, first, qt_next)
        store(i, diagonal(i, carries, sts, qt, 0))
        return sts_first_next, qt_next

    qt = queries_t(1)
    lax.fori_loop(1, n_q, q_tile, (scores(0, first, qt), qt))


def _attention(q, k, v, qb, kb):
    bsz, seq, _ = q.shape
    n_tiles = (seq - N_META) // ATTN_TILE
    pair_blk = pl.BlockSpec((1, seq, LANES), lambda b, p: (b, 0, p))
    bias_blk = pl.BlockSpec((1, seq, LANES), lambda b, p: (b, 0, 0))
    return pl.pallas_call(
        _attn_kernel,
        grid=(bsz, D_ATTN // LANES),
        in_specs=[pair_blk, pair_blk, pair_blk, bias_blk, bias_blk],
        out_specs=pair_blk,
        out_shape=jax.ShapeDtypeStruct((bsz, seq, D_ATTN), bf16),
        scratch_shapes=[pltpu.VMEM((n_tiles, LANES, ATTN_TILE), bf16),
                        pltpu.VMEM((LANES, LANES), bf16)],
        compiler_params=pltpu.CompilerParams(
            dimension_semantics=("arbitrary", "arbitrary"),
            vmem_limit_bytes=VMEM_LIMIT),
        name="attention",
    )(q, k, v, qb, kb)


def _head_norm_block(y, g):
    lo_lanes = lax.broadcasted_iota(jnp.int32, y.shape, 1) < HEAD_DIM
    y2 = y * y
    s_lo = jnp.sum(jnp.where(lo_lanes, y2, 0.0), axis=-1, keepdims=True)
    s_hi = jnp.sum(jnp.where(lo_lanes, 0.0, y2), axis=-1, keepdims=True)
    r_lo = lax.rsqrt(s_lo * (1.0 / HEAD_DIM) + RMS_EPS)
    r_hi = lax.rsqrt(s_hi * (1.0 / HEAD_DIM) + RMS_EPS)
    return y * jnp.where(lo_lanes, r_lo, r_hi) * g


def _mixer_out_kernel(h_ref, ya_ref, a_ref, cx_ref, sb_ref, dw_w_ref, dw_b_ref,
                      ln_g_ref, ln_b_ref, pw_ref, scw_ref, hg_ref, wo_ref, pg_ref,
                      o_ref, a_ext, cx_ext, a_sh, conv_ref, w_bc):
    tm = h_ref.shape[1]

    @pl.when(pl.program_id(1) == 0)
    def _():
        a_ext[0:CONV_HALO, :] = jnp.zeros((CONV_HALO, D_CONV), f32)
        cx_ext[0:SHORT_HALO, :] = jnp.zeros((SHORT_HALO, D_SC), f32)

    @pl.when(pl.program_id(1) > 0)
    def _():
        a_ext[0:CONV_HALO, :] = a_ext[tm:tm + CONV_HALO, :]
        cx_ext[0:SHORT_HALO, :] = cx_ext[tm:tm + SHORT_HALO, :]

    a_ext[CONV_HALO:CONV_HALO + tm, :] = a_ref[0].astype(f32)
    cx_ext[SHORT_HALO:SHORT_HALO + tm, :] = cx_ref[0].astype(f32)

    span = tm + CONV_HALO - SUBLANES
    for r in range(1, SUBLANES):
        a_sh[r - 1] = a_ext[r:r + span, :]
    for t in range(CONV_KERNEL):
        w_bc[t] = jnp.broadcast_to(dw_w_ref[t:t + 1, :], (SUBLANES, D_CONV))

    def conv_rows(rb, _):
        base = pl.multiple_of(rb * CONV_ROWS, CONV_ROWS)
        acc = jnp.broadcast_to(dw_b_ref[...], (CONV_ROWS, D_CONV))
        first_off = CONV_HALO - (CONV_KERNEL - 1)
        for r in range(SUBLANES):
            ms = [off // SUBLANES for off in range(first_off, CONV_HALO + 1)
                  if off % SUBLANES == r]
            rows = pl.ds(pl.multiple_of(base + ms[0] * SUBLANES, SUBLANES),
                         (ms[-1] - ms[0]) * SUBLANES + CONV_ROWS)
            window = a_ext[rows, :] if r == 0 else a_sh[r - 1, rows, :]
            for m in ms:
                t = m * SUBLANES + r - first_off
                lo = (m - ms[0]) * SUBLANES
                w = jnp.concatenate([w_bc[t]] * (CONV_ROWS // SUBLANES), axis=0)
                acc = acc + w * window[lo:lo + CONV_ROWS, :]
        conv_ref[pl.ds(base, CONV_ROWS), :] = acc
        return 0

    lax.fori_loop(0, tm // CONV_ROWS, conv_rows, 0)
    conv = conv_ref[...]
    mu = jnp.mean(conv, axis=-1, keepdims=True)
    cen = conv - mu
    var = jnp.mean(cen * cen, axis=-1, keepdims=True)
    ln = cen * lax.rsqrt(var + LN_EPS) * ln_g_ref[...] + ln_b_ref[...]
    act = ln * jax.nn.sigmoid(ln)
    y_conv = jnp.dot(act.astype(bf16), pw_ref[...], preferred_element_type=f32)

    sc = jnp.zeros((tm, D_SC), f32)
    for t in range(SC_KERNEL):
        off = SHORT_HALO - (SC_KERNEL - 1) + t
        sc = sc + scw_ref[t:t + 1, :] * cx_ext[off:off + tm, :]
    y_sc = sb_ref[0].astype(f32) * sc

    blocks = []
    for c in range(D_MODEL // LANES):
        lo = c * LANES
        if lo < D_ATTN:
            yb = ya_ref[0, :, lo:lo + LANES].astype(f32)
        elif lo < D_ATTN + D_CONV:
            yb = y_conv[:, lo - D_ATTN:lo - D_ATTN + LANES]
        else:
            yb = y_sc[:, lo - D_ATTN - D_CONV:lo - D_ATTN - D_CONV + LANES]
        blocks.append(_head_norm_block(yb, hg_ref[:, lo:lo + LANES]).astype(bf16))
    yn = jnp.concatenate(blocks, axis=-1)
    y = jnp.dot(yn, wo_ref[...], preferred_element_type=f32)
    o_ref[0] = h_ref[0] + _rms_norm(y, pg_ref[...])


def _mixer_out(h, y_attn, a, cx, sb, dw_w, dw_b, ln_g, ln_b, pw, scw, head_g, w_out, post_g):
    bsz, seq, d = h.shape
    n_tiles = seq // SEQ_TILE
    tile = lambda n: pl.BlockSpec((1, SEQ_TILE, n), lambda b, j: (b, j, 0))
    whole = lambda x: pl.BlockSpec(x.shape, lambda b, j: (0,) * x.ndim)
    params = (dw_w, dw_b, ln_g, ln_b, pw, scw, head_g, w_out, post_g)
    return pl.pallas_call(
        _mixer_out_kernel,
        grid=(bsz, n_tiles),
        in_specs=[tile(d), tile(D_ATTN), tile(D_CONV), tile(D_SC), tile(D_SC)]
                 + [whole(x) for x in params],
        out_specs=tile(d),
        out_shape=jax.ShapeDtypeStruct(h.shape, h.dtype),
        scratch_shapes=[pltpu.VMEM((CONV_HALO + SEQ_TILE, D_CONV), f32),
                        pltpu.VMEM((SHORT_HALO + SEQ_TILE, D_SC), f32),
                        pltpu.VMEM((SUBLANES - 1, SEQ_TILE + CONV_HALO - SUBLANES, D_CONV), f32),
                        pltpu.VMEM((SEQ_TILE, D_CONV), f32),
                        pltpu.VMEM((CONV_KERNEL, SUBLANES, D_CONV), f32)],
        compiler_params=pltpu.CompilerParams(
            dimension_semantics=("arbitrary", "arbitrary"),
            vmem_limit_bytes=VMEM_LIMIT),
        name="mixer_out",
    )(h, y_attn, a, cx, sb, *params)


def _ffn_kernel(h_ref, pre_g_ref, wg_ref, wu_ref, cw_ref, wd_ref, post_g_ref,
                o_ref, u_ref, z_ext, halo_ref, acc_ref):
    tm = h_ref.shape[1]
    n_chunks = wg_ref.shape[0]
    fc = wg_ref.shape[2]

    @pl.when(pl.program_id(1) == 0)
    def _():
        halo_ref[...] = jnp.zeros_like(halo_ref)

    u_ref[...] = _rms_norm(h_ref[0], pre_g_ref[...]).astype(bf16)
    acc_ref[...] = jnp.zeros_like(acc_ref)

    def up(c, slot):
        u = u_ref[...]
        z_ext[slot, 0:SHORT_HALO, :] = halo_ref[c]
        z_ext[slot, SHORT_HALO:SHORT_HALO + tm, 0:fc] = jnp.dot(
            u, wg_ref[c], preferred_element_type=f32)
        z_ext[slot, SHORT_HALO:SHORT_HALO + tm, fc:2 * fc] = jnp.dot(
            u, wu_ref[c], preferred_element_type=f32)
        halo_ref[c] = z_ext[slot, tm:tm + SHORT_HALO, :]

    def down(c, slot):
        w = cw_ref[c]
        conv = jnp.zeros((tm, 2 * fc), f32)
        for t in range(FFN_KERNEL):
            off = SHORT_HALO - (FFN_KERNEL - 1) + t
            conv = conv + w[t:t + 1, :] * z_ext[slot, off:off + tm, :]
        g = conv[:, 0:fc]
        act = (g * jax.nn.sigmoid(g) * conv[:, fc:2 * fc]).astype(bf16)
        acc_ref[...] += jnp.dot(act, wd_ref[c], preferred_element_type=f32)

    assert n_chunks % 2 == 1
    up(0, 0)

    def chunk_pair(k, _):
        c = 2 * k
        up(c + 1, 1)
        down(c, 0)
        up(c + 2, 0)
        down(c + 1, 1)
        return 0

    lax.fori_loop(0, n_chunks // 2, chunk_pair, 0)
    down(n_chunks - 1, 0)
    o_ref[0] = h_ref[0] + _rms_norm(acc_ref[...], post_g_ref[...])


def _ffn(h, pre_g, wg, wu, cw, wd, post_g):
    bsz, seq, d = h.shape
    n_tiles = seq // SEQ_TILE
    n_chunks, _, fc = wg.shape
    tile = pl.BlockSpec((1, SEQ_TILE, d), lambda b, j: (b, j, 0))
    whole = lambda x: pl.BlockSpec(x.shape, lambda b, j: (0,) * x.ndim)
    params = (pre_g, wg, wu, cw, wd, post_g)
    return pl.pallas_call(
        _ffn_kernel,
        grid=(bsz, n_tiles),
        in_specs=[tile] + [whole(x) for x in params],
        out_specs=tile,
        out_shape=jax.ShapeDtypeStruct(h.shape, h.dtype),
        scratch_shapes=[pltpu.VMEM((SEQ_TILE, d), bf16),
                        pltpu.VMEM((2, SHORT_HALO + SEQ_TILE, 2 * fc), f32),
                        pltpu.VMEM((n_chunks, SHORT_HALO, 2 * fc), f32),
                        pltpu.VMEM((SEQ_TILE, d), f32)],
        compiler_params=pltpu.CompilerParams(
            dimension_semantics=("arbitrary", "arbitrary"),
            vmem_limit_bytes=VMEM_LIMIT),
        name="ffn",
    )(h, *params)


def _prep_w_in(w_in, b_forget):
    q, k, v, f, a_val, a_gate, sc_b, sc_c, sc_x = jnp.split(
        w_in, [512, 1024, 1536, 1544, 1800, 2056, 2312, 2568], axis=-1)
    pad = LANES - ATTN_HEADS * BIAS_LANES
    f_rep = jnp.pad(jnp.repeat(f, BIAS_LANES, axis=-1), ((0, 0), (0, pad)))
    w = jnp.concatenate([q, k, v, a_val, a_gate, sc_b, sc_c, sc_x, f_rep], axis=-1)
    bf = jnp.pad(jnp.repeat(b_forget, BIAS_LANES), (0, pad))[None, :]
    return w.astype(bf16), bf.astype(f32)


def _chunk_cols(w, n_chunks):
    k, n = w.shape
    return w.reshape(k, n_chunks, n // n_chunks).transpose(1, 0, 2)


def _prep_ffn(w_up, conv_w, w_down):
    n_chunks = D_FF // FF_CHUNK
    wg = _chunk_cols(w_up[:, :D_FF], n_chunks).astype(bf16)
    wu = _chunk_cols(w_up[:, D_FF:], n_chunks).astype(bf16)
    cg = _chunk_cols(conv_w[:, :D_FF], n_chunks)
    cu = _chunk_cols(conv_w[:, D_FF:], n_chunks)
    cw = jnp.concatenate([cg, cu], axis=-1)
    cw = jnp.pad(cw, ((0, 0), (0, SUBLANES - FFN_KERNEL), (0, 0)))
    wd = w_down.reshape(n_chunks, FF_CHUNK, D_MODEL).astype(bf16)
    return wg, wu, cw, wd


def kernel(x, meta_tokens, mix_pre_g, mix_post_g, w_in, b_forget, a_dw_w, a_dw_b, a_ln_g, a_ln_b, a_pw_w, sc_conv_w, head_g, w_out, ffn_pre_g, ffn_post_g, ffn_w_up, ffn_conv_w, ffn_w_down):
    bsz = x.shape[0]
    depth = w_in.shape[0]
    meta = jnp.broadcast_to(meta_tokens[None].astype(x.dtype), (bsz, N_META, x.shape[-1]))
    h = jnp.concatenate([meta, x], axis=1)
    row = lambda p: p[None, :]
    for l in range(depth):
        w_in_r, bf_r = _prep_w_in(w_in[l], b_forget[l])
        q, k, v, a, cx, sb, qb, kb = _in_proj(h, row(mix_pre_g[l]), w_in_r, bf_r)
        y_attn = _attention(q, k, v, qb, kb)
        h = _mixer_out(h, y_attn, a, cx, sb, a_dw_w[l], row(a_dw_b[l]), row(a_ln_g[l]),
                       row(a_ln_b[l]), a_pw_w[l].astype(bf16), sc_conv_w[l],
                       row(head_g[l]), w_out[l].astype(bf16), row(mix_post_g[l]))
        wg, wu, cw, wd = _prep_ffn(ffn_w_up[l], ffn_conv_w[l], ffn_w_down[l])
        h = _ffn(h, row(ffn_pre_g[l]), wg, wu, cw, wd, row(ffn_post_g[l]))
    return h[:, N_META:]
```

```python
import functools

import jax
import jax.numpy as jnp
from jax import lax
from jax.experimental import pallas as pl
from jax.experimental.pallas import tpu as pltpu

D_MODEL = 1024
N_META = 16
ATTN_HEADS = 8
HEAD_DIM = 64
D_ATTN = ATTN_HEADS * HEAD_DIM
D_CONV = 256
D_SC = 256
CONV_KERNEL = 31
SC_KERNEL = 3
D_FF = 2816
FFN_KERNEL = 3
RMS_EPS = 1e-6
LN_EPS = 1e-5
NEG_INF = -1e30
LOG2E = 1.4426950408889634

LANES = 128
SUBLANES = 8
BIAS_LANES = 6
SEQ_TILE = 688
ATTN_TILE = 256
ATTN_QUERIES = 512
CONV_HALO = 32
SHORT_HALO = 8
CONV_ROWS = 16
FF_CHUNK = 256
VMEM_LIMIT = 56 * 1024 * 1024

_OFF_Q = 0
_OFF_K = _OFF_Q + D_ATTN
_OFF_V = _OFF_K + D_ATTN
_OFF_AVAL = _OFF_V + D_ATTN
_OFF_AGATE = _OFF_AVAL + D_CONV
_OFF_SB = _OFF_AGATE + D_CONV
_OFF_SC = _OFF_SB + D_SC
_OFF_SX = _OFF_SC + D_SC
_OFF_F = _OFF_SX + D_SC
D_IN_PAD = _OFF_F + LANES

f32 = jnp.float32
bf16 = jnp.bfloat16


def _rms_norm(x, g):
    ms = jnp.mean(x * x, axis=-1, keepdims=True)
    return x * lax.rsqrt(ms + RMS_EPS) * g


def _cumsum_rows(x):
    n = x.shape[0]
    row = lax.broadcasted_iota(jnp.int32, x.shape, 0)
    shift = 1
    while shift < n:
        x = x + jnp.where(row >= shift, pltpu.roll(x, shift, axis=0), 0.0)
        shift *= 2
    return x


def _in_proj_kernel(h_ref, g_ref, w_ref, bf_ref, q_ref, k_ref, v_ref, a_ref,
                    cx_ref, sb_ref, qb_ref, kb_ref, carry_ref):
    tm = h_ref.shape[1]

    @pl.when(pl.program_id(1) == 0)
    def _():
        carry_ref[...] = jnp.zeros_like(carry_ref)

    u = _rms_norm(h_ref[0], g_ref[...]).astype(bf16)

    def seg(lo, n):
        return jnp.dot(u, w_ref[:, lo:lo + n], preferred_element_type=f32)

    q_ref[0] = (seg(_OFF_Q, D_ATTN) * (HEAD_DIM ** -0.5 * LOG2E)).astype(bf16)
    k_ref[0] = seg(_OFF_K, D_ATTN).astype(bf16)
    v_ref[0] = seg(_OFF_V, D_ATTN).astype(bf16)
    a_ref[0] = (seg(_OFF_AVAL, D_CONV) * jax.nn.sigmoid(seg(_OFF_AGATE, D_CONV))).astype(bf16)
    sb_ref[0] = seg(_OFF_SB, D_SC).astype(bf16)
    cx_ref[0] = (seg(_OFF_SC, D_SC) * seg(_OFF_SX, D_SC)).astype(bf16)

    fl = seg(_OFF_F, LANES) + bf_ref[...]
    log_f = jnp.minimum(fl, 0.0) - jnp.log1p(jnp.exp(-jnp.abs(fl)))
    cum = _cumsum_rows(log_f) + carry_ref[0:1, :]
    carry_ref[...] = jnp.broadcast_to(cum[tm - 1:tm, :], carry_ref.shape)

    cum2 = cum * LOG2E
    hi = cum2.astype(bf16)
    r1 = cum2 - hi.astype(f32)
    mid = r1.astype(bf16)
    lo = (r1 - mid.astype(f32)).astype(bf16)
    lane = lax.broadcasted_iota(jnp.int32, cum.shape, 1)
    piece = lane % BIAS_LANES
    used = lane < ATTN_HEADS * BIAS_LANES
    pieces = jnp.where(piece % 3 == 0, hi, jnp.where(piece % 3 == 1, mid, lo)).astype(f32)
    qb = jnp.where(piece < 3, pieces, 1.0)
    kb = jnp.where(piece < 3, 1.0, -pieces)
    qb_ref[0] = jnp.where(used, qb, 0.0).astype(bf16)
    kb_ref[0] = jnp.where(used, kb, 0.0).astype(bf16)


def _in_proj(h, pre_g, w_in_r, bf_r):
    bsz, seq, d = h.shape
    n_tiles = seq // SEQ_TILE
    tile = lambda n: pl.BlockSpec((1, SEQ_TILE, n), lambda b, j: (b, j, 0))
    whole = lambda a: pl.BlockSpec(a.shape, lambda b, j: (0,) * a.ndim)
    out = lambda n: jax.ShapeDtypeStruct((bsz, seq, n), bf16)
    return pl.pallas_call(
        _in_proj_kernel,
        grid=(bsz, n_tiles),
        in_specs=[tile(d), whole(pre_g), whole(w_in_r), whole(bf_r)],
        out_specs=[tile(D_ATTN), tile(D_ATTN), tile(D_ATTN), tile(D_CONV),
                   tile(D_SC), tile(D_SC), tile(LANES), tile(LANES)],
        out_shape=[out(D_ATTN), out(D_ATTN), out(D_ATTN), out(D_CONV),
                   out(D_SC), out(D_SC), out(LANES), out(LANES)],
        scratch_shapes=[pltpu.VMEM((SUBLANES, LANES), f32)],
        compiler_params=pltpu.CompilerParams(
            dimension_semantics=("arbitrary", "arbitrary"),
            vmem_limit_bytes=VMEM_LIMIT),
        name="in_proj",
    )(h, pre_g, w_in_r, bf_r)


def _attn_kernel(q_ref, k_ref, v_ref, qb_ref, kb_ref, o_ref, vt_ref, vtm_ref,
                 qt_ref, st_ref, sf_ref):
    seq = q_ref.shape[1]
    n_tiles = (seq - N_META) // ATTN_TILE
    pair = pl.program_id(1)
    lane = lax.broadcasted_iota(jnp.int32, (1, LANES), 1)
    nt_dims = (((1,), (1,)), ((), ()))

    q_masks, b_masks = [], []
    for hh in range(2):
        head = 2 * pair + hh
        q_masks.append((lane // HEAD_DIM == hh).astype(bf16))
        b_masks.append(((lane >= BIAS_LANES * head) & (lane < BIAS_LANES * (head + 1))).astype(bf16))

    def queries(start, size, hh):
        return jnp.concatenate([q_ref[0, pl.ds(start, size), :] * q_masks[hh],
                                qb_ref[0, pl.ds(start, size), :] * b_masks[hh]], axis=1)

    def keys(start, size):
        return jnp.concatenate([k_ref[0, pl.ds(start, size), :],
                                kb_ref[0, pl.ds(start, size), :]], axis=1)

    v_meta = jnp.concatenate([v_ref[0, 0:N_META, :].astype(f32),
                              jnp.zeros((LANES - N_META, LANES), f32)], axis=0)
    vtm_ref[...] = v_meta.T.astype(bf16)
    for j in range(n_tiles):
        lo = N_META + j * ATTN_TILE
        vt_ref[j] = v_ref[0, lo:lo + ATTN_TILE, :].astype(f32).T.astype(bf16)

    lo_lanes = lax.broadcasted_iota(jnp.int32, (N_META, LANES), 1) < HEAD_DIM
    k_meta = keys(0, N_META)
    o_meta = []
    for hh in range(2):
        s = lax.dot_general(queries(0, N_META, hh), k_meta, nt_dims, preferred_element_type=f32)
        r = lax.broadcasted_iota(jnp.int32, s.shape, 0)
        c = lax.broadcasted_iota(jnp.int32, s.shape, 1)
        s = jnp.where(r >= c, s, NEG_INF)
        p = jnp.exp2(s - jnp.max(s, axis=-1, keepdims=True))
        o = jnp.dot(p.astype(bf16), v_ref[0, 0:N_META, :], preferred_element_type=f32)
        o_meta.append(o * (1.0 / jnp.sum(p, axis=-1, keepdims=True)))
    o_ref[0, 0:N_META, :] = jnp.where(lo_lanes, o_meta[0], o_meta[1]).astype(bf16)

    first = N_META + ATTN_TILE
    n_q = (seq - N_META) // ATTN_QUERIES
    k_per_q = ATTN_QUERIES // ATTN_TILE

    def put_queries(i, slot):
        start = N_META + i * ATTN_QUERIES
        for hh in range(2):
            qt_ref[slot, hh] = queries(start, ATTN_QUERIES, hh).astype(f32).T.astype(bf16)

    def put_scores(dst, start, size, q_slot):
        kj = keys(start, size)
        for hh in range(2):
            dst[hh] = jnp.dot(kj, qt_ref[q_slot, hh], preferred_element_type=f32)

    def put_tile_scores(j, q_slot):
        put_scores(st_ref.at[j % 2], N_META + j * ATTN_TILE, ATTN_TILE, q_slot)

    def first_block(hh, mask):
        st = sf_ref[hh]
        if mask is not None:
            st = jnp.where(mask, st, NEG_INF)
        m = jnp.max(st, axis=0, keepdims=True)
        pt = jnp.exp2(st - m)
        l = jnp.sum(pt, axis=0, keepdims=True)
        pt = pt.astype(bf16)
        pt_meta = jnp.concatenate(
            [pt[0:N_META], jnp.zeros((LANES - N_META, ATTN_QUERIES), bf16)], axis=0)
        head = slice(hh * HEAD_DIM, (hh + 1) * HEAD_DIM)
        acc = (jnp.dot(vt_ref[0, head, :], pt[N_META:first], preferred_element_type=f32)
               + jnp.dot(vtm_ref[head, :], pt_meta, preferred_element_type=f32))
        return m, l, acc

    def update(carry, j, hh, mask):
        m, l, acc = carry
        st = st_ref[j % 2, hh]
        if mask is not None:
            st = jnp.where(mask, st, NEG_INF)
        m_new = jnp.maximum(m, jnp.max(st, axis=0, keepdims=True))
        alpha = jnp.exp2(m - m_new)
        pt = jnp.exp2(st - m_new)
        l = alpha * l + jnp.sum(pt, axis=0, keepdims=True)
        vt_h = vt_ref[j, hh * HEAD_DIM:(hh + 1) * HEAD_DIM, :]
        acc = alpha * acc + jnp.dot(vt_h, pt.astype(bf16), preferred_element_type=f32)
        return m_new, l, acc

    def store(i, carries):
        o_t = jnp.concatenate([acc * (1.0 / l) for _, l, acc in carries], axis=0)
        start = N_META + i * ATTN_QUERIES
        o_ref[0, start:start + ATTN_QUERIES, :] = o_t.T.astype(bf16)

    kk = lax.broadcasted_iota(jnp.int32, (ATTN_TILE, ATTN_QUERIES), 0)
    qq = lax.broadcasted_iota(jnp.int32, (ATTN_TILE, ATTN_QUERIES), 1)
    kk_first = lax.broadcasted_iota(jnp.int32, (first, ATTN_QUERIES), 0)
    qq_first = lax.broadcasted_iota(jnp.int32, (first, ATTN_QUERIES), 1)

    put_queries(0, 0)
    put_scores(sf_ref, 0, first, 0)
    for i in range(n_q):
        q_slot = i % 2
        n_full = i * k_per_q
        put_tile_scores(1, q_slot)
        if i == 0:
            carries = [first_block(hh, qq_first >= kk_first - N_META) for hh in range(2)]
            diag = range(1, k_per_q)
        else:
            carries = [first_block(hh, None) for hh in range(2)]
            for j in range(1, n_full):
                put_tile_scores(j + 1, q_slot)
                carries = [update(carries[hh], j, hh, None) for hh in range(2)]
            diag = range(k_per_q)
        for t in diag:
            j = n_full + t
            if t + 1 < k_per_q:
                put_tile_scores(j + 1, q_slot)
            elif i + 1 < n_q:
                put_queries(i + 1, 1 - q_slot)
                put_scores(sf_ref, 0, first, 1 - q_slot)
            mask = qq >= kk + t * ATTN_TILE
            carries = [update(carries[hh], j, hh, mask) for hh in range(2)]
        store(i, carries)


def _attention(q, k, v, qb, kb):
    bsz, seq, _ = q.shape
    n_tiles = (seq - N_META) // ATTN_TILE
    pair_blk = pl.BlockSpec((1, seq, LANES), lambda b, p: (b, 0, p))
    bias_blk = pl.BlockSpec((1, seq, LANES), lambda b, p: (b, 0, 0))
    return pl.pallas_call(
        _attn_kernel,
        grid=(bsz, D_ATTN // LANES),
        in_specs=[pair_blk, pair_blk, pair_blk, bias_blk, bias_blk],
        out_specs=pair_blk,
        out_shape=jax.ShapeDtypeStruct((bsz, seq, D_ATTN), bf16),
        scratch_shapes=[pltpu.VMEM((n_tiles, LANES, ATTN_TILE), bf16),
                        pltpu.VMEM((LANES, LANES), bf16),
                        pltpu.VMEM((2, 2, 2 * LANES, ATTN_QUERIES), bf16),
                        pltpu.VMEM((2, 2, ATTN_TILE, ATTN_QUERIES), f32),
                        pltpu.VMEM((2, N_META + ATTN_TILE, ATTN_QUERIES), f32)],
        compiler_params=pltpu.CompilerParams(
            dimension_semantics=("arbitrary", "arbitrary"),
            vmem_limit_bytes=VMEM_LIMIT),
        name="attention",
    )(q, k, v, qb, kb)


def _head_norm_block(y, g):
    lo_lanes = lax.broadcasted_iota(jnp.int32, y.shape, 1) < HEAD_DIM
    y2 = y * y
    s_lo = jnp.sum(jnp.where(lo_lanes, y2, 0.0), axis=-1, keepdims=True)
    s_hi = jnp.sum(jnp.where(lo_lanes, 0.0, y2), axis=-1, keepdims=True)
    r_lo = lax.rsqrt(s_lo * (1.0 / HEAD_DIM) + RMS_EPS)
    r_hi = lax.rsqrt(s_hi * (1.0 / HEAD_DIM) + RMS_EPS)
    return y * jnp.where(lo_lanes, r_lo, r_hi) * g


def _mixer_out_kernel(h_ref, ya_ref, a_ref, cx_ref, sb_ref, dw_w_ref, dw_b_ref,
                      ln_g_ref, ln_b_ref, pw_ref, scw_ref, hg_ref, wo_ref, pg_ref,
                      o_ref, a_ext, cx_ext, a_sh, conv_ref, w_bc):
    tm = h_ref.shape[1]

    @pl.when(pl.program_id(1) == 0)
    def _():
        a_ext[0:CONV_HALO, :] = jnp.zeros((CONV_HALO, D_CONV), f32)
        cx_ext[0:SHORT_HALO, :] = jnp.zeros((SHORT_HALO, D_SC), f32)

    @pl.when(pl.program_id(1) > 0)
    def _():
        a_ext[0:CONV_HALO, :] = a_ext[tm:tm + CONV_HALO, :]
        cx_ext[0:SHORT_HALO, :] = cx_ext[tm:tm + SHORT_HALO, :]

    a_ext[CONV_HALO:CONV_HALO + tm, :] = a_ref[0].astype(f32)
    cx_ext[SHORT_HALO:SHORT_HALO + tm, :] = cx_ref[0].astype(f32)

    span = tm + CONV_HALO - SUBLANES
    for r in range(1, SUBLANES):
        a_sh[r - 1] = a_ext[r:r + span, :]
    for t in range(CONV_KERNEL):
        w_bc[t] = jnp.broadcast_to(dw_w_ref[t:t + 1, :], (SUBLANES, D_CONV))

    def conv_rows(rb, _):
        base = pl.multiple_of(rb * CONV_ROWS, CONV_ROWS)
        acc = jnp.broadcast_to(dw_b_ref[...], (CONV_ROWS, D_CONV))
        first_off = CONV_HALO - (CONV_KERNEL - 1)
        for r in range(SUBLANES):
            ms = [off // SUBLANES for off in range(first_off, CONV_HALO + 1)
                  if off % SUBLANES == r]
            rows = pl.ds(pl.multiple_of(base + ms[0] * SUBLANES, SUBLANES),
                         (ms[-1] - ms[0]) * SUBLANES + CONV_ROWS)
            window = a_ext[rows, :] if r == 0 else a_sh[r - 1, rows, :]
            for m in ms:
                t = m * SUBLANES + r - first_off
                lo = (m - ms[0]) * SUBLANES
                w = jnp.concatenate([w_bc[t]] * (CONV_ROWS // SUBLANES), axis=0)
                acc = acc + w * window[lo:lo + CONV_ROWS, :]
        conv_ref[pl.ds(base, CONV_ROWS), :] = acc
        return 0

    lax.fori_loop(0, tm // CONV_ROWS, conv_rows, 0)
    conv = conv_ref[...]
    mu = jnp.mean(conv, axis=-1, keepdims=True)
    cen = conv - mu
    var = jnp.mean(cen * cen, axis=-1, keepdims=True)
    ln = cen * lax.rsqrt(var + LN_EPS) * ln_g_ref[...] + ln_b_ref[...]
    act = ln * jax.nn.sigmoid(ln)
    y_conv = jnp.dot(act.astype(bf16), pw_ref[...], preferred_element_type=f32)

    sc = jnp.zeros((tm, D_SC), f32)
    for t in range(SC_KERNEL):
        off = SHORT_HALO - (SC_KERNEL - 1) + t
        sc = sc + scw_ref[t:t + 1, :] * cx_ext[off:off + tm, :]
    y_sc = sb_ref[0].astype(f32) * sc

    blocks = []
    for c in range(D_MODEL // LANES):
        lo = c * LANES
        if lo < D_ATTN:
            yb = ya_ref[0, :, lo:lo + LANES].astype(f32)
        elif lo < D_ATTN + D_CONV:
            yb = y_conv[:, lo - D_ATTN:lo - D_ATTN + LANES]
        else:
            yb = y_sc[:, lo - D_ATTN - D_CONV:lo - D_ATTN - D_CONV + LANES]
        blocks.append(_head_norm_block(yb, hg_ref[:, lo:lo + LANES]).astype(bf16))
    yn = jnp.concatenate(blocks, axis=-1)
    y = jnp.dot(yn, wo_ref[...], preferred_element_type=f32)
    o_ref[0] = h_ref[0] + _rms_norm(y, pg_ref[...])


def _mixer_out(h, y_attn, a, cx, sb, dw_w, dw_b, ln_g, ln_b, pw, scw, head_g, w_out, post_g):
    bsz, seq, d = h.shape
    n_tiles = seq // SEQ_TILE
    tile = lambda n: pl.BlockSpec((1, SEQ_TILE, n), lambda b, j: (b, j, 0))
    whole = lambda x: pl.BlockSpec(x.shape, lambda b, j: (0,) * x.ndim)
    params = (dw_w, dw_b, ln_g, ln_b, pw, scw, head_g, w_out, post_g)
    return pl.pallas_call(
        _mixer_out_kernel,
        grid=(bsz, n_tiles),
        in_specs=[tile(d), tile(D_ATTN), tile(D_CONV), tile(D_SC), tile(D_SC)]
                 + [whole(x) for x in params],
        out_specs=tile(d),
        out_shape=jax.ShapeDtypeStruct(h.shape, h.dtype),
        scratch_shapes=[pltpu.VMEM((CONV_HALO + SEQ_TILE, D_CONV), f32),
                        pltpu.VMEM((SHORT_HALO + SEQ_TILE, D_SC), f32),
                        pltpu.VMEM((SUBLANES - 1, SEQ_TILE + CONV_HALO - SUBLANES, D_CONV), f32),
                        pltpu.VMEM((SEQ_TILE, D_CONV), f32),
                        pltpu.VMEM((CONV_KERNEL, SUBLANES, D_CONV), f32)],
        compiler_params=pltpu.CompilerParams(
            dimension_semantics=("arbitrary", "arbitrary"),
            vmem_limit_bytes=VMEM_LIMIT),
        name="mixer_out",
    )(h, y_attn, a, cx, sb, *params)


def _ffn_kernel(h_ref, pre_g_ref, wg_ref, wu_ref, cw_ref, wd_ref, post_g_ref,
                o_ref, u_ref, z_ext, halo_ref, acc_ref):
    tm = h_ref.shape[1]
    n_chunks = wg_ref.shape[0]
    fc = wg_ref.shape[2]

    @pl.when(pl.program_id(1) == 0)
    def _():
        halo_ref[...] = jnp.zeros_like(halo_ref)

    u_ref[...] = _rms_norm(h_ref[0], pre_g_ref[...]).astype(bf16)
    acc_ref[...] = jnp.zeros_like(acc_ref)

    def up(c, slot):
        u = u_ref[...]
        z_ext[slot, 0:SHORT_HALO, :] = halo_ref[c]
        z_ext[slot, SHORT_HALO:SHORT_HALO + tm, 0:fc] = jnp.dot(
            u, wg_ref[c], preferred_element_type=f32)
        z_ext[slot, SHORT_HALO:SHORT_HALO + tm, fc:2 * fc] = jnp.dot(
            u, wu_ref[c], preferred_element_type=f32)
        halo_ref[c] = z_ext[slot, tm:tm + SHORT_HALO, :]

    def down(c, slot):
        w = cw_ref[c]
        conv = jnp.zeros((tm, 2 * fc), f32)
        for t in range(FFN_KERNEL):
            off = SHORT_HALO - (FFN_KERNEL - 1) + t
            conv = conv + w[t:t + 1, :] * z_ext[slot, off:off + tm, :]
        g = conv[:, 0:fc]
        act = (g * jax.nn.sigmoid(g) * conv[:, fc:2 * fc]).astype(bf16)
        acc_ref[...] += jnp.dot(act, wd_ref[c], preferred_element_type=f32)

    assert n_chunks % 2 == 1
    up(0, 0)

    def chunk_pair(k, _):
        c = 2 * k
        up(c + 1, 1)
        down(c, 0)
        up(c + 2, 0)
        down(c + 1, 1)
        return 0

    lax.fori_loop(0, n_chunks // 2, chunk_pair, 0)
    down(n_chunks - 1, 0)
    o_ref[0] = h_ref[0] + _rms_norm(acc_ref[...], post_g_ref[...])


def _ffn(h, pre_g, wg, wu, cw, wd, post_g):
    bsz, seq, d = h.shape
    n_tiles = seq // SEQ_TILE
    n_chunks, _, fc = wg.shape
    tile = pl.BlockSpec((1, SEQ_TILE, d), lambda b, j: (b, j, 0))
    whole = lambda x: pl.BlockSpec(x.shape, lambda b, j: (0,) * x.ndim)
    params = (pre_g, wg, wu, cw, wd, post_g)
    return pl.pallas_call(
        _ffn_kernel,
        grid=(bsz, n_tiles),
        in_specs=[tile] + [whole(x) for x in params],
        out_specs=tile,
        out_shape=jax.ShapeDtypeStruct(h.shape, h.dtype),
        scratch_shapes=[pltpu.VMEM((SEQ_TILE, d), bf16),
                        pltpu.VMEM((2, SHORT_HALO + SEQ_TILE, 2 * fc), f32),
                        pltpu.VMEM((n_chunks, SHORT_HALO, 2 * fc), f32),
                        pltpu.VMEM((SEQ_TILE, d), f32)],
        compiler_params=pltpu.CompilerParams(
            dimension_semantics=("arbitrary", "arbitrary"),
            vmem_limit_bytes=VMEM_LIMIT),
        name="ffn",
    )(h, *params)


def _prep_w_in(w_in, b_forget):
    q, k, v, f, a_val, a_gate, sc_b, sc_c, sc_x = jnp.split(
        w_in, [512, 1024, 1536, 1544, 1800, 2056, 2312, 2568], axis=-1)
    pad = LANES - ATTN_HEADS * BIAS_LANES
    f_rep = jnp.pad(jnp.repeat(f, BIAS_LANES, axis=-1), ((0, 0), (0, pad)))
    w = jnp.concatenate([q, k, v, a_val, a_gate, sc_b, sc_c, sc_x, f_rep], axis=-1)
    bf = jnp.pad(jnp.repeat(b_forget, BIAS_LANES), (0, pad))[None, :]
    return w.astype(bf16), bf.astype(f32)


def _chunk_cols(w, n_chunks):
    k, n = w.shape
    return w.reshape(k, n_chunks, n // n_chunks).transpose(1, 0, 2)


def _prep_ffn(w_up, conv_w, w_down):
    n_chunks = D_FF // FF_CHUNK
    wg = _chunk_cols(w_up[:, :D_FF], n_chunks).astype(bf16)
    wu = _chunk_cols(w_up[:, D_FF:], n_chunks).astype(bf16)
    cg = _chunk_cols(conv_w[:, :D_FF], n_chunks)
    cu = _chunk_cols(conv_w[:, D_FF:], n_chunks)
    cw = jnp.concatenate([cg, cu], axis=-1)
    cw = jnp.pad(cw, ((0, 0), (0, SUBLANES - FFN_KERNEL), (0, 0)))
    wd = w_down.reshape(n_chunks, FF_CHUNK, D_MODEL).astype(bf16)
    return wg, wu, cw, wd


def kernel(x, meta_tokens, mix_pre_g, mix_post_g, w_in, b_forget, a_dw_w, a_dw_b, a_ln_g, a_ln_b, a_pw_w, sc_conv_w, head_g, w_out, ffn_pre_g, ffn_post_g, ffn_w_up, ffn_conv_w, ffn_w_down):
    bsz = x.shape[0]
    depth = w_in.shape[0]
    meta = jnp.broadcast_to(meta_tokens[None].astype(x.dtype), (bsz, N_META, x.shape[-1]))
    h = jnp.concatenate([meta, x], axis=1)
    row = lambda p: p[None, :]
    for l in range(depth):
        w_in_r, bf_r = _prep_w_in(w_in[l], b_forget[l])
        q, k, v, a, cx, sb, qb, kb = _in_proj(h, row(mix_pre_g[l]), w_in_r, bf_r)
        y_attn = _attention(q, k, v, qb, kb)
        h = _mixer_out(h, y_attn, a, cx, sb, a_dw_w[l], row(a_dw_b[l]), row(a_ln_g[l]),
                       row(a_ln_b[l]), a_pw_w[l].astype(bf16), sc_conv_w[l],
                       row(head_g[l]), w_out[l].astype(bf16), row(mix_post_g[l]))
        wg, wu, cw, wd = _prep_ffn(ffn_w_up[l], ffn_conv_w[l], ffn_w_down[l])
        h = _ffn(h, row(ffn_pre_g[l]), wg, wu, cw, wd, row(ffn_post_g[l]))
    return h[:, N_META:]
```

```python
import functools

import jax
import jax.numpy as jnp
from jax import lax
from jax.experimental import pallas as pl
from jax.experimental.pallas import tpu as pltpu

D_MODEL = 1024
N_META = 16
ATTN_HEADS = 8
HEAD_DIM = 64
D_ATTN = ATTN_HEADS * HEAD_DIM
D_CONV = 256
D_SC = 256
CONV_KERNEL = 31
SC_KERNEL = 3
D_FF = 2816
FFN_KERNEL = 3
RMS_EPS = 1e-6
LN_EPS = 1e-5
NEG_INF = -1e30
LOG2E = 1.4426950408889634

LANES = 128
SUBLANES = 8
BIAS_LANES = 6
SEQ_TILE = 688
ATTN_TILE = 256
ATTN_QUERIES = 512
SUM_ROWS = 16
CONV_HALO = 32
SHORT_HALO = 8
CONV_ROWS = 16
FF_CHUNK = 256
VMEM_LIMIT = 56 * 1024 * 1024

_OFF_Q = 0
_OFF_K = _OFF_Q + D_ATTN
_OFF_V = _OFF_K + D_ATTN
_OFF_AVAL = _OFF_V + D_ATTN
_OFF_AGATE = _OFF_AVAL + D_CONV
_OFF_SB = _OFF_AGATE + D_CONV
_OFF_SC = _OFF_SB + D_SC
_OFF_SX = _OFF_SC + D_SC
_OFF_F = _OFF_SX + D_SC
D_IN_PAD = _OFF_F + LANES

f32 = jnp.float32
bf16 = jnp.bfloat16


def _rms_norm(x, g):
    ms = jnp.mean(x * x, axis=-1, keepdims=True)
    return x * lax.rsqrt(ms + RMS_EPS) * g


def _cumsum_rows(x):
    n = x.shape[0]
    row = lax.broadcasted_iota(jnp.int32, x.shape, 0)
    shift = 1
    while shift < n:
        x = x + jnp.where(row >= shift, pltpu.roll(x, shift, axis=0), 0.0)
        shift *= 2
    return x


def _in_proj_kernel(h_ref, g_ref, w_ref, bf_ref, q_ref, k_ref, v_ref, a_ref,
                    cx_ref, sb_ref, qb_ref, kb_ref, carry_ref):
    tm = h_ref.shape[1]

    @pl.when(pl.program_id(1) == 0)
    def _():
        carry_ref[...] = jnp.zeros_like(carry_ref)

    u = _rms_norm(h_ref[0], g_ref[...]).astype(bf16)

    def seg(lo, n):
        return jnp.dot(u, w_ref[:, lo:lo + n], preferred_element_type=f32)

    q_ref[0] = (seg(_OFF_Q, D_ATTN) * (HEAD_DIM ** -0.5 * LOG2E)).astype(bf16)
    k_ref[0] = seg(_OFF_K, D_ATTN).astype(bf16)
    v_ref[0] = seg(_OFF_V, D_ATTN).astype(bf16)
    a_ref[0] = (seg(_OFF_AVAL, D_CONV) * jax.nn.sigmoid(seg(_OFF_AGATE, D_CONV))).astype(bf16)
    sb_ref[0] = seg(_OFF_SB, D_SC).astype(bf16)
    cx_ref[0] = (seg(_OFF_SC, D_SC) * seg(_OFF_SX, D_SC)).astype(bf16)

    fl = seg(_OFF_F, LANES) + bf_ref[...]
    log_f = jnp.minimum(fl, 0.0) - jnp.log1p(jnp.exp(-jnp.abs(fl)))
    cum = _cumsum_rows(log_f) + carry_ref[0:1, :]
    carry_ref[...] = jnp.broadcast_to(cum[tm - 1:tm, :], carry_ref.shape)

    cum2 = cum * LOG2E
    hi = cum2.astype(bf16)
    r1 = cum2 - hi.astype(f32)
    mid = r1.astype(bf16)
    lo = (r1 - mid.astype(f32)).astype(bf16)
    lane = lax.broadcasted_iota(jnp.int32, cum.shape, 1)
    piece = lane % BIAS_LANES
    used = lane < ATTN_HEADS * BIAS_LANES
    pieces = jnp.where(piece % 3 == 0, hi, jnp.where(piece % 3 == 1, mid, lo)).astype(f32)
    qb = jnp.where(piece < 3, pieces, 1.0)
    kb = jnp.where(piece < 3, 1.0, -pieces)
    qb_ref[0] = jnp.where(used, qb, 0.0).astype(bf16)
    kb_ref[0] = jnp.where(used, kb, 0.0).astype(bf16)


def _in_proj(h, pre_g, w_in_r, bf_r):
    bsz, seq, d = h.shape
    n_tiles = seq // SEQ_TILE
    tile = lambda n: pl.BlockSpec((1, SEQ_TILE, n), lambda b, j: (b, j, 0))
    whole = lambda a: pl.BlockSpec(a.shape, lambda b, j: (0,) * a.ndim)
    out = lambda n: jax.ShapeDtypeStruct((bsz, seq, n), bf16)
    return pl.pallas_call(
        _in_proj_kernel,
        grid=(bsz, n_tiles),
        in_specs=[tile(d), whole(pre_g), whole(w_in_r), whole(bf_r)],
        out_specs=[tile(D_ATTN), tile(D_ATTN), tile(D_ATTN), tile(D_CONV),
                   tile(D_SC), tile(D_SC), tile(LANES), tile(LANES)],
        out_shape=[out(D_ATTN), out(D_ATTN), out(D_ATTN), out(D_CONV),
                   out(D_SC), out(D_SC), out(LANES), out(LANES)],
        scratch_shapes=[pltpu.VMEM((SUBLANES, LANES), f32)],
        compiler_params=pltpu.CompilerParams(
            dimension_semantics=("arbitrary", "arbitrary"),
            vmem_limit_bytes=VMEM_LIMIT),
        name="in_proj",
    )(h, pre_g, w_in_r, bf_r)


def _attn_kernel(q_ref, k_ref, v_ref, qb_ref, kb_ref, o_ref, vt_ref, vtm_ref,
                 qt_ref, st_ref, sf_ref):
    seq = q_ref.shape[1]
    n_tiles = (seq - N_META) // ATTN_TILE
    pair = pl.program_id(1)
    lane = lax.broadcasted_iota(jnp.int32, (1, LANES), 1)
    nt_dims = (((1,), (1,)), ((), ()))

    q_masks, b_masks = [], []
    for hh in range(2):
        head = 2 * pair + hh
        q_masks.append((lane // HEAD_DIM == hh).astype(bf16))
        b_masks.append(((lane >= BIAS_LANES * head) & (lane < BIAS_LANES * (head + 1))).astype(bf16))

    def queries(start, size, hh):
        return jnp.concatenate([q_ref[0, pl.ds(start, size), :] * q_masks[hh],
                                qb_ref[0, pl.ds(start, size), :] * b_masks[hh]], axis=1)

    def keys(start, size):
        return jnp.concatenate([k_ref[0, pl.ds(start, size), :],
                                kb_ref[0, pl.ds(start, size), :]], axis=1)

    def with_ones(vt_pair):
        ones = jnp.ones((SUM_ROWS, vt_pair.shape[1]), bf16)
        return [jnp.concatenate([vt_pair[hh * HEAD_DIM:(hh + 1) * HEAD_DIM], ones], axis=0)
                for hh in range(2)]

    v_meta = jnp.concatenate([v_ref[0, 0:N_META, :].astype(f32),
                              jnp.zeros((LANES - N_META, LANES), f32)], axis=0)
    for hh, vt_h in enumerate(with_ones(v_meta.T.astype(bf16))):
        vtm_ref[hh] = vt_h
    for j in range(n_tiles):
        lo = N_META + j * ATTN_TILE
        vt_pair = v_ref[0, lo:lo + ATTN_TILE, :].astype(f32).T.astype(bf16)
        for hh, vt_h in enumerate(with_ones(vt_pair)):
            vt_ref[j, hh] = vt_h

    lo_lanes = lax.broadcasted_iota(jnp.int32, (N_META, LANES), 1) < HEAD_DIM
    k_meta = keys(0, N_META)
    o_meta = []
    for hh in range(2):
        s = lax.dot_general(queries(0, N_META, hh), k_meta, nt_dims, preferred_element_type=f32)
        r = lax.broadcasted_iota(jnp.int32, s.shape, 0)
        c = lax.broadcasted_iota(jnp.int32, s.shape, 1)
        s = jnp.where(r >= c, s, NEG_INF)
        p = jnp.exp2(s - jnp.max(s, axis=-1, keepdims=True))
        o = jnp.dot(p.astype(bf16), v_ref[0, 0:N_META, :], preferred_element_type=f32)
        o_meta.append(o * (1.0 / jnp.sum(p, axis=-1, keepdims=True)))
    o_ref[0, 0:N_META, :] = jnp.where(lo_lanes, o_meta[0], o_meta[1]).astype(bf16)

    first = N_META + ATTN_TILE
    n_q = (seq - N_META) // ATTN_QUERIES
    k_per_q = ATTN_QUERIES // ATTN_TILE

    def put_queries(i, slot):
        start = N_META + i * ATTN_QUERIES
        for hh in range(2):
            qt_ref[slot, hh] = queries(start, ATTN_QUERIES, hh).astype(f32).T.astype(bf16)

    def put_scores(dst, start, size, q_slot):
        kj = keys(start, size)
        for hh in range(2):
            dst[hh] = jnp.dot(kj, qt_ref[q_slot, hh], preferred_element_type=f32)

    def put_tile_scores(j, q_slot):
        put_scores(st_ref.at[j % 2], N_META + j * ATTN_TILE, ATTN_TILE, q_slot)

    def first_block(hh, mask):
        st = sf_ref[hh]
        if mask is not None:
            st = jnp.where(mask, st, NEG_INF)
        m = jnp.max(st, axis=0, keepdims=True)
        pt = jnp.exp2(st - m).astype(bf16)
        pt_meta = jnp.concatenate(
            [pt[0:N_META], jnp.zeros((LANES - N_META, ATTN_QUERIES), bf16)], axis=0)
        acc = (jnp.dot(vt_ref[0, hh], pt[N_META:first], preferred_element_type=f32)
               + jnp.dot(vtm_ref[hh], pt_meta, preferred_element_type=f32))
        return m, acc

    def update(carry, j, hh, mask):
        m, acc = carry
        st = st_ref[j % 2, hh]
        if mask is not None:
            st = jnp.where(mask, st, NEG_INF)
        m_new = jnp.maximum(m, jnp.max(st, axis=0, keepdims=True))
        alpha = jnp.exp2(m - m_new)
        pt = jnp.exp2(st - m_new).astype(bf16)
        acc = alpha * acc + jnp.dot(vt_ref[j, hh], pt, preferred_element_type=f32)
        return m_new, acc

    def store(i, carries):
        o_t = jnp.concatenate(
            [acc[0:HEAD_DIM] * (1.0 / acc[HEAD_DIM:HEAD_DIM + 1]) for _, acc in carries], axis=0)
        start = N_META + i * ATTN_QUERIES
        o_ref[0, start:start + ATTN_QUERIES, :] = o_t.T.astype(bf16)

    kk = lax.broadcasted_iota(jnp.int32, (ATTN_TILE, ATTN_QUERIES), 0)
    qq = lax.broadcasted_iota(jnp.int32, (ATTN_TILE, ATTN_QUERIES), 1)
    kk_first = lax.broadcasted_iota(jnp.int32, (first, ATTN_QUERIES), 0)
    qq_first = lax.broadcasted_iota(jnp.int32, (first, ATTN_QUERIES), 1)

    put_queries(0, 0)
    put_scores(sf_ref, 0, first, 0)
    for i in range(n_q):
        q_slot = i % 2
        n_full = i * k_per_q
        put_tile_scores(1, q_slot)
        if i == 0:
            carries = [first_block(hh, qq_first >= kk_first - N_META) for hh in range(2)]
            diag = range(1, k_per_q)
        else:
            carries = [first_block(hh, None) for hh in range(2)]
            for j in range(1, n_full):
                put_tile_scores(j + 1, q_slot)
                carries = [update(carries[hh], j, hh, None) for hh in range(2)]
            diag = range(k_per_q)
        for t in diag:
            j = n_full + t
            if t + 1 < k_per_q:
                put_tile_scores(j + 1, q_slot)
            elif i + 1 < n_q:
                put_queries(i + 1, 1 - q_slot)
                put_scores(sf_ref, 0, first, 1 - q_slot)
            mask = qq >= kk + t * ATTN_TILE
            carries = [update(carries[hh], j, hh, mask) for hh in range(2)]
        store(i, carries)


def _attention(q, k, v, qb, kb):
    bsz, seq, _ = q.shape
    n_tiles = (seq - N_META) // ATTN_TILE
    pair_blk = pl.BlockSpec((1, seq, LANES), lambda b, p: (b, 0, p))
    bias_blk = pl.BlockSpec((1, seq, LANES), lambda b, p: (b, 0, 0))
    return pl.pallas_call(
        _attn_kernel,
        grid=(bsz, D_ATTN // LANES),
        in_specs=[pair_blk, pair_blk, pair_blk, bias_blk, bias_blk],
        out_specs=pair_blk,
        out_shape=jax.ShapeDtypeStruct((bsz, seq, D_ATTN), bf16),
        scratch_shapes=[pltpu.VMEM((n_tiles, 2, HEAD_DIM + SUM_ROWS, ATTN_TILE), bf16),
                        pltpu.VMEM((2, HEAD_DIM + SUM_ROWS, LANES), bf16),
                        pltpu.VMEM((2, 2, 2 * LANES, ATTN_QUERIES), bf16),
                        pltpu.VMEM((2, 2, ATTN_TILE, ATTN_QUERIES), f32),
                        pltpu.VMEM((2, N_META + ATTN_TILE, ATTN_QUERIES), f32)],
        compiler_params=pltpu.CompilerParams(
            dimension_semantics=("arbitrary", "arbitrary"),
            vmem_limit_bytes=VMEM_LIMIT),
        name="attention",
    )(q, k, v, qb, kb)


def _head_norm_block(y, g):
    lo_lanes = lax.broadcasted_iota(jnp.int32, y.shape, 1) < HEAD_DIM
    y2 = y * y
    s_lo = jnp.sum(jnp.where(lo_lanes, y2, 0.0), axis=-1, keepdims=True)
    s_hi = jnp.sum(jnp.where(lo_lanes, 0.0, y2), axis=-1, keepdims=True)
    r_lo = lax.rsqrt(s_lo * (1.0 / HEAD_DIM) + RMS_EPS)
    r_hi = lax.rsqrt(s_hi * (1.0 / HEAD_DIM) + RMS_EPS)
    return y * jnp.where(lo_lanes, r_lo, r_hi) * g


def _mixer_out_kernel(h_ref, ya_ref, a_ref, cx_ref, sb_ref, dw_w_ref, dw_b_ref,
                      ln_g_ref, ln_b_ref, pw_ref, scw_ref, hg_ref, wo_ref, pg_ref,
                      o_ref, a_ext, cx_ext, a_sh, conv_ref, w_bc):
    tm = h_ref.shape[1]

    @pl.when(pl.program_id(1) == 0)
    def _():
        a_ext[0:CONV_HALO, :] = jnp.zeros((CONV_HALO, D_CONV), f32)
        cx_ext[0:SHORT_HALO, :] = jnp.zeros((SHORT_HALO, D_SC), f32)

    @pl.when(pl.program_id(1) > 0)
    def _():
        a_ext[0:CONV_HALO, :] = a_ext[tm:tm + CONV_HALO, :]
        cx_ext[0:SHORT_HALO, :] = cx_ext[tm:tm + SHORT_HALO, :]

    a_ext[CONV_HALO:CONV_HALO + tm, :] = a_ref[0].astype(f32)
    cx_ext[SHORT_HALO:SHORT_HALO + tm, :] = cx_ref[0].astype(f32)

    span = tm + CONV_HALO - SUBLANES
    for r in range(1, SUBLANES):
        a_sh[r - 1] = a_ext[r:r + span, :]
    for t in range(CONV_KERNEL):
        w_bc[t] = jnp.broadcast_to(dw_w_ref[t:t + 1, :], (SUBLANES, D_CONV))

    def conv_rows(rb, _):
        base = pl.multiple_of(rb * CONV_ROWS, CONV_ROWS)
        acc = jnp.broadcast_to(dw_b_ref[...], (CONV_ROWS, D_CONV))
        first_off = CONV_HALO - (CONV_KERNEL - 1)
        for r in range(SUBLANES):
            ms = [off // SUBLANES for off in range(first_off, CONV_HALO + 1)
                  if off % SUBLANES == r]
            rows = pl.ds(pl.multiple_of(base + ms[0] * SUBLANES, SUBLANES),
                         (ms[-1] - ms[0]) * SUBLANES + CONV_ROWS)
            window = a_ext[rows, :] if r == 0 else a_sh[r - 1, rows, :]
            for m in ms:
                t = m * SUBLANES + r - first_off
                lo = (m - ms[0]) * SUBLANES
                w = jnp.concatenate([w_bc[t]] * (CONV_ROWS // SUBLANES), axis=0)
                acc = acc + w * window[lo:lo + CONV_ROWS, :]
        conv_ref[pl.ds(base, CONV_ROWS), :] = acc
        return 0

    lax.fori_loop(0, tm // CONV_ROWS, conv_rows, 0)
    conv = conv_ref[...]
    mu = jnp.mean(conv, axis=-1, keepdims=True)
    cen = conv - mu
    var = jnp.mean(cen * cen, axis=-1, keepdims=True)
    ln = cen * lax.rsqrt(var + LN_EPS) * ln_g_ref[...] + ln_b_ref[...]
    act = ln * jax.nn.sigmoid(ln)
    y_conv = jnp.dot(act.astype(bf16), pw_ref[...], preferred_element_type=f32)

    sc = jnp.zeros((tm, D_SC), f32)
    for t in range(SC_KERNEL):
        off = SHORT_HALO - (SC_KERNEL - 1) + t
        sc = sc + scw_ref[t:t + 1, :] * cx_ext[off:off + tm, :]
    y_sc = sb_ref[0].astype(f32) * sc

    blocks = []
    for c in range(D_MODEL // LANES):
        lo = c * LANES
        if lo < D_ATTN:
            yb = ya_ref[0, :, lo:lo + LANES].astype(f32)
        elif lo < D_ATTN + D_CONV:
            yb = y_conv[:, lo - D_ATTN:lo - D_ATTN + LANES]
        else:
            yb = y_sc[:, lo - D_ATTN - D_CONV:lo - D_ATTN - D_CONV + LANES]
        blocks.append(_head_norm_block(yb, hg_ref[:, lo:lo + LANES]).astype(bf16))
    yn = jnp.concatenate(blocks, axis=-1)
    y = jnp.dot(yn, wo_ref[...], preferred_element_type=f32)
    o_ref[0] = h_ref[0] + _rms_norm(y, pg_ref[...])


def _mixer_out(h, y_attn, a, cx, sb, dw_w, dw_b, ln_g, ln_b, pw, scw, head_g, w_out, post_g):
    bsz, seq, d = h.shape
    n_tiles = seq // SEQ_TILE
    tile = lambda n: pl.BlockSpec((1, SEQ_TILE, n), lambda b, j: (b, j, 0))
    whole = lambda x: pl.BlockSpec(x.shape, lambda b, j: (0,) * x.ndim)
    params = (dw_w, dw_b, ln_g, ln_b, pw, scw, head_g, w_out, post_g)
    return pl.pallas_call(
        _mixer_out_kernel,
        grid=(bsz, n_tiles),
        in_specs=[tile(d), tile(D_ATTN), tile(D_CONV), tile(D_SC), tile(D_SC)]
                 + [whole(x) for x in params],
        out_specs=tile(d),
        out_shape=jax.ShapeDtypeStruct(h.shape, h.dtype),
        scratch_shapes=[pltpu.VMEM((CONV_HALO + SEQ_TILE, D_CONV), f32),
                        pltpu.VMEM((SHORT_HALO + SEQ_TILE, D_SC), f32),
                        pltpu.VMEM((SUBLANES - 1, SEQ_TILE + CONV_HALO - SUBLANES, D_CONV), f32),
                        pltpu.VMEM((SEQ_TILE, D_CONV), f32),
                        pltpu.VMEM((CONV_KERNEL, SUBLANES, D_CONV), f32)],
        compiler_params=pltpu.CompilerParams(
            dimension_semantics=("arbitrary", "arbitrary"),
            vmem_limit_bytes=VMEM_LIMIT),
        name="mixer_out",
    )(h, y_attn, a, cx, sb, *params)


def _ffn_kernel(h_ref, pre_g_ref, wg_ref, wu_ref, cw_ref, wd_ref, post_g_ref,
                o_ref, u_ref, z_ext, halo_ref, acc_ref, *out_scratch, drop_meta):
    tm = h_ref.shape[1]
    n_chunks = wg_ref.shape[0]
    fc = wg_ref.shape[2]

    @pl.when(pl.program_id(1) == 0)
    def _():
        halo_ref[...] = jnp.zeros_like(halo_ref)

    u_ref[...] = _rms_norm(h_ref[0], pre_g_ref[...]).astype(bf16)
    acc_ref[...] = jnp.zeros_like(acc_ref)

    def up(c, slot):
        u = u_ref[...]
        z_ext[slot, 0:SHORT_HALO, :] = halo_ref[c]
        z_ext[slot, SHORT_HALO:SHORT_HALO + tm, 0:fc] = jnp.dot(
            u, wg_ref[c], preferred_element_type=f32)
        z_ext[slot, SHORT_HALO:SHORT_HALO + tm, fc:2 * fc] = jnp.dot(
            u, wu_ref[c], preferred_element_type=f32)
        halo_ref[c] = z_ext[slot, tm:tm + SHORT_HALO, :]

    def down(c, slot):
        w = cw_ref[c]
        conv = jnp.zeros((tm, 2 * fc), f32)
        for t in range(FFN_KERNEL):
            off = SHORT_HALO - (FFN_KERNEL - 1) + t
            conv = conv + w[t:t + 1, :] * z_ext[slot, off:off + tm, :]
        g = conv[:, 0:fc]
        act = (g * jax.nn.sigmoid(g) * conv[:, fc:2 * fc]).astype(bf16)
        acc_ref[...] += jnp.dot(act, wd_ref[c], preferred_element_type=f32)

    assert n_chunks % 2 == 1
    up(0, 0)

    def chunk_pair(k, _):
        c = 2 * k
        up(c + 1, 1)
        down(c, 0)
        up(c + 2, 0)
        down(c + 1, 1)
        return 0

    lax.fori_loop(0, n_chunks // 2, chunk_pair, 0)
    down(n_chunks - 1, 0)
    res = h_ref[0] + _rms_norm(acc_ref[...], post_g_ref[...])
    if not drop_meta:
        o_ref[0] = res
        return

    res_ref, sem = out_scratch
    b, j = pl.program_id(0), pl.program_id(1)
    step = b * pl.num_programs(1) + j
    last_step = pl.num_programs(0) * pl.num_programs(1) - 1

    def body_copy(jj):
        return pltpu.make_async_copy(
            res_ref.at[pl.ds(N_META, tm - N_META)],
            o_ref.at[b, pl.ds(jj * tm, tm - N_META)], sem.at[0])

    def head_copy(jj):
        return pltpu.make_async_copy(
            res_ref.at[pl.ds(0, N_META)],
            o_ref.at[b, pl.ds(jj * tm - N_META, N_META)], sem.at[1])

    @pl.when(step > 0)
    def _():
        body_copy(j).wait()

    @pl.when((step > 0) & (j != 1))
    def _():
        head_copy(jnp.maximum(j, 1)).wait()

    res_ref[...] = res
    body_copy(j).start()

    @pl.when(j > 0)
    def _():
        head_copy(j).start()

    @pl.when(step == last_step)
    def _():
        body_copy(j).wait()
        head_copy(j).wait()


def _ffn(h, pre_g, wg, wu, cw, wd, post_g, drop_meta=False):
    bsz, seq, d = h.shape
    n_tiles = seq // SEQ_TILE
    n_chunks, _, fc = wg.shape
    tile = pl.BlockSpec((1, SEQ_TILE, d), lambda b, j: (b, j, 0))
    whole = lambda x: pl.BlockSpec(x.shape, lambda b, j: (0,) * x.ndim)
    params = (pre_g, wg, wu, cw, wd, post_g)
    if drop_meta:
        assert n_tiles >= 2
        out_spec = pl.BlockSpec(memory_space=pl.ANY)
        out_shape = jax.ShapeDtypeStruct((bsz, seq - N_META, d), h.dtype)
        out_scratch = [pltpu.VMEM((SEQ_TILE, d), f32), pltpu.SemaphoreType.DMA((2,))]
    else:
        out_spec, out_shape, out_scratch = tile, jax.ShapeDtypeStruct(h.shape, h.dtype), []
    return pl.pallas_call(
        functools.partial(_ffn_kernel, drop_meta=drop_meta),
        grid=(bsz, n_tiles),
        in_specs=[tile] + [whole(x) for x in params],
        out_specs=out_spec,
        out_shape=out_shape,
        scratch_shapes=[pltpu.VMEM((SEQ_TILE, d), bf16),
                        pltpu.VMEM((2, SHORT_HALO + SEQ_TILE, 2 * fc), f32),
                        pltpu.VMEM((n_chunks, SHORT_HALO, 2 * fc), f32),
                        pltpu.VMEM((SEQ_TILE, d), f32)] + out_scratch,
        compiler_params=pltpu.CompilerParams(
            dimension_semantics=("arbitrary", "arbitrary"),
            vmem_limit_bytes=VMEM_LIMIT),
        name="ffn",
    )(h, *params)


def _prep_w_in(w_in, b_forget):
    q, k, v, f, a_val, a_gate, sc_b, sc_c, sc_x = jnp.split(
        w_in, [512, 1024, 1536, 1544, 1800, 2056, 2312, 2568], axis=-1)
    pad = LANES - ATTN_HEADS * BIAS_LANES
    f_rep = jnp.pad(jnp.repeat(f, BIAS_LANES, axis=-1), ((0, 0), (0, pad)))
    w = jnp.concatenate([q, k, v, a_val, a_gate, sc_b, sc_c, sc_x, f_rep], axis=-1)
    bf = jnp.pad(jnp.repeat(b_forget, BIAS_LANES), (0, pad))[None, :]
    return w.astype(bf16), bf.astype(f32)


def _chunk_cols(w, n_chunks):
    k, n = w.shape
    return w.reshape(k, n_chunks, n // n_chunks).transpose(1, 0, 2)


def _prep_ffn(w_up, conv_w, w_down):
    n_chunks = D_FF // FF_CHUNK
    wg = _chunk_cols(w_up[:, :D_FF], n_chunks).astype(bf16)
    wu = _chunk_cols(w_up[:, D_FF:], n_chunks).astype(bf16)
    cg = _chunk_cols(conv_w[:, :D_FF], n_chunks)
    cu = _chunk_cols(conv_w[:, D_FF:], n_chunks)
    cw = jnp.concatenate([cg, cu], axis=-1)
    cw = jnp.pad(cw, ((0, 0), (0, SUBLANES - FFN_KERNEL), (0, 0)))
    wd = w_down.reshape(n_chunks, FF_CHUNK, D_MODEL).astype(bf16)
    return wg, wu, cw, wd


def kernel(x, meta_tokens, mix_pre_g, mix_post_g, w_in, b_forget, a_dw_w, a_dw_b, a_ln_g, a_ln_b, a_pw_w, sc_conv_w, head_g, w_out, ffn_pre_g, ffn_post_g, ffn_w_up, ffn_conv_w, ffn_w_down):
    bsz = x.shape[0]
    depth = w_in.shape[0]
    meta = jnp.broadcast_to(meta_tokens[None].astype(x.dtype), (bsz, N_META, x.shape[-1]))
    h = jnp.concatenate([meta, x], axis=1)
    row = lambda p: p[None, :]
    for l in range(depth):
        w_in_r, bf_r = _prep_w_in(w_in[l], b_forget[l])
        q, k, v, a, cx, sb, qb, kb = _in_proj(h, row(mix_pre_g[l]), w_in_r, bf_r)
        y_attn = _attention(q, k, v, qb, kb)
        h = _mixer_out(h, y_attn, a, cx, sb, a_dw_w[l], row(a_dw_b[l]), row(a_ln_g[l]),
                       row(a_ln_b[l]), a_pw_w[l].astype(bf16), sc_conv_w[l],
                       row(head_g[l]), w_out[l].astype(bf16), row(mix_post_g[l]))
        wg, wu, cw, wd = _prep_ffn(ffn_w_up[l], ffn_conv_w[l], ffn_w_down[l])
        h = _ffn(h, row(ffn_pre_g[l]), wg, wu, cw, wd, row(ffn_post_g[l]),
                 drop_meta=(l == depth - 1))
    return h
```

```python
import functools

import jax
import jax.numpy as jnp
from jax import lax
from jax.experimental import pallas as pl
from jax.experimental.pallas import tpu as pltpu

D_MODEL = 1024
N_META = 16
ATTN_HEADS = 8
HEAD_DIM = 64
D_ATTN = ATTN_HEADS * HEAD_DIM
D_CONV = 256
D_SC = 256
CONV_KERNEL = 31
SC_KERNEL = 3
D_FF = 2816
FFN_KERNEL = 3
RMS_EPS = 1e-6
LN_EPS = 1e-5
NEG_INF = -1e30
LOG2E = 1.4426950408889634

LANES = 128
SUBLANES = 8
BIAS_LANES = 6
SEQ_TILE = 688
ATTN_TILE = 256
ATTN_QUERIES = 512
SUM_ROWS = 16
CONV_HALO = 32
SHORT_HALO = 8
CONV_ROWS = 16
FF_CHUNK = 256
VMEM_LIMIT = 56 * 1024 * 1024

_OFF_Q = 0
_OFF_K = _OFF_Q + D_ATTN
_OFF_V = _OFF_K + D_ATTN
_OFF_AVAL = _OFF_V + D_ATTN
_OFF_AGATE = _OFF_AVAL + D_CONV
_OFF_SB = _OFF_AGATE + D_CONV
_OFF_SC = _OFF_SB + D_SC
_OFF_SX = _OFF_SC + D_SC
_OFF_F = _OFF_SX + D_SC
D_IN_PAD = _OFF_F + LANES

f32 = jnp.float32
bf16 = jnp.bfloat16


def _rms_norm(x, g):
    ms = jnp.mean(x * x, axis=-1, keepdims=True)
    return x * lax.rsqrt(ms + RMS_EPS) * g


def _cumsum_rows(x):
    n = x.shape[0]
    row = lax.broadcasted_iota(jnp.int32, x.shape, 0)
    shift = 1
    while shift < n:
        x = x + jnp.where(row >= shift, pltpu.roll(x, shift, axis=0), 0.0)
        shift *= 2
    return x


def _in_proj_kernel(*refs, prepend_meta):
    if prepend_meta:
        (x_ref, x_prev_ref, meta_ref, g_ref, w_ref, bf_ref, h0_ref, q_ref, k_ref, v_ref,
         a_ref, cx_ref, sb_ref, qb_ref, kb_ref, carry_ref) = refs
        tm = x_ref.shape[1]
        head = jnp.where(pl.program_id(1) == 0, meta_ref[...], x_prev_ref[0])
        h = jnp.concatenate([head, x_ref[0, 0:tm - N_META, :]], axis=0)
        h0_ref[0] = h
    else:
        (h_ref, g_ref, w_ref, bf_ref, q_ref, k_ref, v_ref,
         a_ref, cx_ref, sb_ref, qb_ref, kb_ref, carry_ref) = refs
        tm = h_ref.shape[1]
        h = h_ref[0]

    @pl.when(pl.program_id(1) == 0)
    def _():
        carry_ref[...] = jnp.zeros_like(carry_ref)

    u = _rms_norm(h, g_ref[...]).astype(bf16)

    def seg(lo, n):
        return jnp.dot(u, w_ref[:, lo:lo + n], preferred_element_type=f32)

    q_ref[0] = (seg(_OFF_Q, D_ATTN) * (HEAD_DIM ** -0.5 * LOG2E)).astype(bf16)
    k_ref[0] = seg(_OFF_K, D_ATTN).astype(bf16)
    v_ref[0] = seg(_OFF_V, D_ATTN).astype(bf16)
    a_ref[0] = (seg(_OFF_AVAL, D_CONV) * jax.nn.sigmoid(seg(_OFF_AGATE, D_CONV))).astype(bf16)
    sb_ref[0] = seg(_OFF_SB, D_SC).astype(bf16)
    cx_ref[0] = (seg(_OFF_SC, D_SC) * seg(_OFF_SX, D_SC)).astype(bf16)

    fl = seg(_OFF_F, LANES) + bf_ref[...]
    log_f = jnp.minimum(fl, 0.0) - jnp.log1p(jnp.exp(-jnp.abs(fl)))
    cum = _cumsum_rows(log_f) + carry_ref[0:1, :]
    carry_ref[...] = jnp.broadcast_to(cum[tm - 1:tm, :], carry_ref.shape)

    cum2 = cum * LOG2E
    hi = cum2.astype(bf16)
    r1 = cum2 - hi.astype(f32)
    mid = r1.astype(bf16)
    lo = (r1 - mid.astype(f32)).astype(bf16)
    lane = lax.broadcasted_iota(jnp.int32, cum.shape, 1)
    piece = lane % BIAS_LANES
    used = lane < ATTN_HEADS * BIAS_LANES
    pieces = jnp.where(piece % 3 == 0, hi, jnp.where(piece % 3 == 1, mid, lo)).astype(f32)
    qb = jnp.where(piece < 3, pieces, 1.0)
    kb = jnp.where(piece < 3, 1.0, -pieces)
    qb_ref[0] = jnp.where(used, qb, 0.0).astype(bf16)
    kb_ref[0] = jnp.where(used, kb, 0.0).astype(bf16)


def _in_proj(h, pre_g, w_in_r, bf_r, meta=None):
    prepend_meta = meta is not None
    bsz, rows, d = h.shape
    seq = rows + N_META if prepend_meta else rows
    n_tiles = seq // SEQ_TILE
    tile = lambda n: pl.BlockSpec((1, SEQ_TILE, n), lambda b, j: (b, j, 0))
    whole = lambda a: pl.BlockSpec(a.shape, lambda b, j: (0,) * a.ndim)
    out = lambda n: jax.ShapeDtypeStruct((bsz, seq, n), bf16)
    in_specs = [tile(d), whole(pre_g), whole(w_in_r), whole(bf_r)]
    operands = [h, pre_g, w_in_r, bf_r]
    out_specs = [tile(D_ATTN), tile(D_ATTN), tile(D_ATTN), tile(D_CONV),
                 tile(D_SC), tile(D_SC), tile(LANES), tile(LANES)]
    out_shape = [out(D_ATTN), out(D_ATTN), out(D_ATTN), out(D_CONV),
                 out(D_SC), out(D_SC), out(LANES), out(LANES)]
    if prepend_meta:
        per_tile = SEQ_TILE // N_META
        prev_rows = pl.BlockSpec((1, N_META, d),
                                 lambda b, j: (b, jnp.maximum(j * per_tile - 1, 0), 0))
        in_specs = [tile(d), prev_rows, whole(meta)] + in_specs[1:]
        operands = [h, h, meta] + operands[1:]
        out_specs = [tile(d)] + out_specs
        out_shape = [jax.ShapeDtypeStruct((bsz, seq, d), h.dtype)] + out_shape
    return pl.pallas_call(
        functools.partial(_in_proj_kernel, prepend_meta=prepend_meta),
        grid=(bsz, n_tiles),
        in_specs=in_specs,
        out_specs=out_specs,
        out_shape=out_shape,
        scratch_shapes=[pltpu.VMEM((SUBLANES, LANES), f32)],
        compiler_params=pltpu.CompilerParams(
            dimension_semantics=("arbitrary", "arbitrary"),
            vmem_limit_bytes=VMEM_LIMIT),
        name="in_proj",
    )(*operands)


def _attn_kernel(q_ref, k_ref, v_ref, qb_ref, kb_ref, o_ref, vt_ref, vtm_ref,
                 qt_ref, st_ref, sf_ref):
    seq = q_ref.shape[1]
    n_tiles = (seq - N_META) // ATTN_TILE
    pair = pl.program_id(1)
    lane = lax.broadcasted_iota(jnp.int32, (1, LANES), 1)
    nt_dims = (((1,), (1,)), ((), ()))

    q_masks, b_masks = [], []
    for hh in range(2):
        head = 2 * pair + hh
        q_masks.append((lane // HEAD_DIM == hh).astype(bf16))
        b_masks.append(((lane >= BIAS_LANES * head) & (lane < BIAS_LANES * (head + 1))).astype(bf16))

    def queries(start, size, hh):
        return jnp.concatenate([q_ref[0, pl.ds(start, size), :] * q_masks[hh],
                                qb_ref[0, pl.ds(start, size), :] * b_masks[hh]], axis=1)

    def keys(start, size):
        return jnp.concatenate([k_ref[0, pl.ds(start, size), :],
                                kb_ref[0, pl.ds(start, size), :]], axis=1)

    def with_ones(vt_pair):
        ones = jnp.ones((SUM_ROWS, vt_pair.shape[1]), bf16)
        return [jnp.concatenate([vt_pair[hh * HEAD_DIM:(hh + 1) * HEAD_DIM], ones], axis=0)
                for hh in range(2)]

    v_meta = jnp.concatenate([v_ref[0, 0:N_META, :].astype(f32),
                              jnp.zeros((LANES - N_META, LANES), f32)], axis=0)
    for hh, vt_h in enumerate(with_ones(v_meta.T.astype(bf16))):
        vtm_ref[hh] = vt_h
    for j in range(n_tiles):
        lo = N_META + j * ATTN_TILE
        vt_pair = v_ref[0, lo:lo + ATTN_TILE, :].astype(f32).T.astype(bf16)
        for hh, vt_h in enumerate(with_ones(vt_pair)):
            vt_ref[j, hh] = vt_h

    lo_lanes = lax.broadcasted_iota(jnp.int32, (N_META, LANES), 1) < HEAD_DIM
    k_meta = keys(0, N_META)
    o_meta = []
    for hh in range(2):
        s = lax.dot_general(queries(0, N_META, hh), k_meta, nt_dims, preferred_element_type=f32)
        r = lax.broadcasted_iota(jnp.int32, s.shape, 0)
        c = lax.broadcasted_iota(jnp.int32, s.shape, 1)
        s = jnp.where(r >= c, s, NEG_INF)
        p = jnp.exp2(s - jnp.max(s, axis=-1, keepdims=True))
        o = jnp.dot(p.astype(bf16), v_ref[0, 0:N_META, :], preferred_element_type=f32)
        o_meta.append(o * (1.0 / jnp.sum(p, axis=-1, keepdims=True)))
    o_ref[0, 0:N_META, :] = jnp.where(lo_lanes, o_meta[0], o_meta[1]).astype(bf16)

    first = N_META + ATTN_TILE
    n_q = (seq - N_META) // ATTN_QUERIES
    k_per_q = ATTN_QUERIES // ATTN_TILE

    def put_queries(i, slot):
        start = N_META + i * ATTN_QUERIES
        for hh in range(2):
            qt_ref[slot, hh] = queries(start, ATTN_QUERIES, hh).astype(f32).T.astype(bf16)

    def put_scores(dst, start, size, q_slot):
        kj = keys(start, size)
        for hh in range(2):
            dst[hh] = jnp.dot(kj, qt_ref[q_slot, hh], preferred_element_type=f32)

    def put_tile_scores(j, q_slot):
        put_scores(st_ref.at[j % 2], N_META + j * ATTN_TILE, ATTN_TILE, q_slot)

    def first_block(hh, mask):
        st = sf_ref[hh]
        if mask is not None:
            st = jnp.where(mask, st, NEG_INF)
        m = jnp.max(st, axis=0, keepdims=True)
        pt = jnp.exp2(st - m).astype(bf16)
        pt_meta = jnp.concatenate(
            [pt[0:N_META], jnp.zeros((LANES - N_META, ATTN_QUERIES), bf16)], axis=0)
        acc = (jnp.dot(vt_ref[0, hh], pt[N_META:first], preferred_element_type=f32)
               + jnp.dot(vtm_ref[hh], pt_meta, preferred_element_type=f32))
        return m, acc

    def update(carry, j, hh, mask):
        m, acc = carry
        st = st_ref[j % 2, hh]
        if mask is not None:
            st = jnp.where(mask, st, NEG_INF)
        m_new = jnp.maximum(m, jnp.max(st, axis=0, keepdims=True))
        alpha = jnp.exp2(m - m_new)
        pt = jnp.exp2(st - m_new).astype(bf16)
        acc = alpha * acc + jnp.dot(vt_ref[j, hh], pt, preferred_element_type=f32)
        return m_new, acc

    def store(i, carries):
        o_t = jnp.concatenate(
            [acc[0:HEAD_DIM] * (1.0 / acc[HEAD_DIM:HEAD_DIM + 1]) for _, acc in carries], axis=0)
        start = N_META + i * ATTN_QUERIES
        o_ref[0, start:start + ATTN_QUERIES, :] = o_t.T.astype(bf16)

    kk = lax.broadcasted_iota(jnp.int32, (ATTN_TILE, ATTN_QUERIES), 0)
    qq = lax.broadcasted_iota(jnp.int32, (ATTN_TILE, ATTN_QUERIES), 1)
    kk_first = lax.broadcasted_iota(jnp.int32, (first, ATTN_QUERIES), 0)
    qq_first = lax.broadcasted_iota(jnp.int32, (first, ATTN_QUERIES), 1)

    put_queries(0, 0)
    put_scores(sf_ref, 0, first, 0)
    for i in range(n_q):
        q_slot = i % 2
        n_full = i * k_per_q
        put_tile_scores(1, q_slot)
        if i == 0:
            carries = [first_block(hh, qq_first >= kk_first - N_META) for hh in range(2)]
            diag = range(1, k_per_q)
        else:
            carries = [first_block(hh, None) for hh in range(2)]
            for j in range(1, n_full):
                put_tile_scores(j + 1, q_slot)
                carries = [update(carries[hh], j, hh, None) for hh in range(2)]
            diag = range(k_per_q)
        for t in diag:
            j = n_full + t
            if t + 1 < k_per_q:
                put_tile_scores(j + 1, q_slot)
            elif i + 1 < n_q:
                put_queries(i + 1, 1 - q_slot)
                put_scores(sf_ref, 0, first, 1 - q_slot)
            mask = qq >= kk + t * ATTN_TILE
            carries = [update(carries[hh], j, hh, mask) for hh in range(2)]
        store(i, carries)


def _attention(q, k, v, qb, kb):
    bsz, seq, _ = q.shape
    n_tiles = (seq - N_META) // ATTN_TILE
    pair_blk = pl.BlockSpec((1, seq, LANES), lambda b, p: (b, 0, p))
    bias_blk = pl.BlockSpec((1, seq, LANES), lambda b, p: (b, 0, 0))
    return pl.pallas_call(
        _attn_kernel,
        grid=(bsz, D_ATTN // LANES),
        in_specs=[pair_blk, pair_blk, pair_blk, bias_blk, bias_blk],
        out_specs=pair_blk,
        out_shape=jax.ShapeDtypeStruct((bsz, seq, D_ATTN), bf16),
        scratch_shapes=[pltpu.VMEM((n_tiles, 2, HEAD_DIM + SUM_ROWS, ATTN_TILE), bf16),
                        pltpu.VMEM((2, HEAD_DIM + SUM_ROWS, LANES), bf16),
                        pltpu.VMEM((2, 2, 2 * LANES, ATTN_QUERIES), bf16),
                        pltpu.VMEM((2, 2, ATTN_TILE, ATTN_QUERIES), f32),
                        pltpu.VMEM((2, N_META + ATTN_TILE, ATTN_QUERIES), f32)],
        compiler_params=pltpu.CompilerParams(
            dimension_semantics=("arbitrary", "arbitrary"),
            vmem_limit_bytes=VMEM_LIMIT),
        name="attention",
    )(q, k, v, qb, kb)


def _head_norm_block(y, g):
    lo_lanes = lax.broadcasted_iota(jnp.int32, y.shape, 1) < HEAD_DIM
    y2 = y * y
    s_lo = jnp.sum(jnp.where(lo_lanes, y2, 0.0), axis=-1, keepdims=True)
    s_hi = jnp.sum(jnp.where(lo_lanes, 0.0, y2), axis=-1, keepdims=True)
    r_lo = lax.rsqrt(s_lo * (1.0 / HEAD_DIM) + RMS_EPS)
    r_hi = lax.rsqrt(s_hi * (1.0 / HEAD_DIM) + RMS_EPS)
    return y * jnp.where(lo_lanes, r_lo, r_hi) * g


def _mixer_out_kernel(h_ref, ya_ref, a_ref, cx_ref, sb_ref, dw_w_ref, dw_b_ref,
                      ln_g_ref, ln_b_ref, pw_ref, scw_ref, hg_ref, wo_ref, pg_ref,
                      o_ref, a_ext, cx_ext, a_sh, conv_ref, w_bc):
    tm = h_ref.shape[1]

    @pl.when(pl.program_id(1) == 0)
    def _():
        a_ext[0:CONV_HALO, :] = jnp.zeros((CONV_HALO, D_CONV), f32)
        cx_ext[0:SHORT_HALO, :] = jnp.zeros((SHORT_HALO, D_SC), f32)

    @pl.when(pl.program_id(1) > 0)
    def _():
        a_ext[0:CONV_HALO, :] = a_ext[tm:tm + CONV_HALO, :]
        cx_ext[0:SHORT_HALO, :] = cx_ext[tm:tm + SHORT_HALO, :]

    a_ext[CONV_HALO:CONV_HALO + tm, :] = a_ref[0].astype(f32)
    cx_ext[SHORT_HALO:SHORT_HALO + tm, :] = cx_ref[0].astype(f32)

    span = tm + CONV_HALO - SUBLANES
    for r in range(1, SUBLANES):
        a_sh[r - 1] = a_ext[r:r + span, :]
    for t in range(CONV_KERNEL):
        w_bc[t] = jnp.broadcast_to(dw_w_ref[t:t + 1, :], (SUBLANES, D_CONV))

    def conv_rows(rb, _):
        base = pl.multiple_of(rb * CONV_ROWS, CONV_ROWS)
        acc = jnp.broadcast_to(dw_b_ref[...], (CONV_ROWS, D_CONV))
        first_off = CONV_HALO - (CONV_KERNEL - 1)
        for r in range(SUBLANES):
            ms = [off // SUBLANES for off in range(first_off, CONV_HALO + 1)
                  if off % SUBLANES == r]
            rows = pl.ds(pl.multiple_of(base + ms[0] * SUBLANES, SUBLANES),
                         (ms[-1] - ms[0]) * SUBLANES + CONV_ROWS)
            window = a_ext[rows, :] if r == 0 else a_sh[r - 1, rows, :]
            for m in ms:
                t = m * SUBLANES + r - first_off
                lo = (m - ms[0]) * SUBLANES
                w = jnp.concatenate([w_bc[t]] * (CONV_ROWS // SUBLANES), axis=0)
                acc = acc + w * window[lo:lo + CONV_ROWS, :]
        conv_ref[pl.ds(base, CONV_ROWS), :] = acc
        return 0

    lax.fori_loop(0, tm // CONV_ROWS, conv_rows, 0)
    conv = conv_ref[...]
    mu = jnp.mean(conv, axis=-1, keepdims=True)
    cen = conv - mu
    var = jnp.mean(cen * cen, axis=-1, keepdims=True)
    ln = cen * lax.rsqrt(var + LN_EPS) * ln_g_ref[...] + ln_b_ref[...]
    act = ln * jax.nn.sigmoid(ln)
    y_conv = jnp.dot(act.astype(bf16), pw_ref[...], preferred_element_type=f32)

    sc = jnp.zeros((tm, D_SC), f32)
    for t in range(SC_KERNEL):
        off = SHORT_HALO - (SC_KERNEL - 1) + t
        sc = sc + scw_ref[t:t + 1, :] * cx_ext[off:off + tm, :]
    y_sc = sb_ref[0].astype(f32) * sc

    blocks = []
    for c in range(D_MODEL // LANES):
        lo = c * LANES
        if lo < D_ATTN:
            yb = ya_ref[0, :, lo:lo + LANES].astype(f32)
        elif lo < D_ATTN + D_CONV:
            yb = y_conv[:, lo - D_ATTN:lo - D_ATTN + LANES]
        else:
            yb = y_sc[:, lo - D_ATTN - D_CONV:lo - D_ATTN - D_CONV + LANES]
        blocks.append(_head_norm_block(yb, hg_ref[:, lo:lo + LANES]).astype(bf16))
    yn = jnp.concatenate(blocks, axis=-1)
    y = jnp.dot(yn, wo_ref[...], preferred_element_type=f32)
    o_ref[0] = h_ref[0] + _rms_norm(y, pg_ref[...])


def _mixer_out(h, y_attn, a, cx, sb, dw_w, dw_b, ln_g, ln_b, pw, scw, head_g, w_out, post_g):
    bsz, seq, d = h.shape
    n_tiles = seq // SEQ_TILE
    tile = lambda n: pl.BlockSpec((1, SEQ_TILE, n), lambda b, j: (b, j, 0))
    whole = lambda x: pl.BlockSpec(x.shape, lambda b, j: (0,) * x.ndim)
    params = (dw_w, dw_b, ln_g, ln_b, pw, scw, head_g, w_out, post_g)
    return pl.pallas_call(
        _mixer_out_kernel,
        grid=(bsz, n_tiles),
        in_specs=[tile(d), tile(D_ATTN), tile(D_CONV), tile(D_SC), tile(D_SC)]
                 + [whole(x) for x in params],
        out_specs=tile(d),
        out_shape=jax.ShapeDtypeStruct(h.shape, h.dtype),
        scratch_shapes=[pltpu.VMEM((CONV_HALO + SEQ_TILE, D_CONV), f32),
                        pltpu.VMEM((SHORT_HALO + SEQ_TILE, D_SC), f32),
                        pltpu.VMEM((SUBLANES - 1, SEQ_TILE + CONV_HALO - SUBLANES, D_CONV), f32),
                        pltpu.VMEM((SEQ_TILE, D_CONV), f32),
                        pltpu.VMEM((CONV_KERNEL, SUBLANES, D_CONV), f32)],
        compiler_params=pltpu.CompilerParams(
            dimension_semantics=("arbitrary", "arbitrary"),
            vmem_limit_bytes=VMEM_LIMIT),
        name="mixer_out",
    )(h, y_attn, a, cx, sb, *params)


def _ffn_kernel(h_ref, pre_g_ref, wg_ref, wu_ref, cw_ref, wd_ref, post_g_ref,
                o_ref, u_ref, z_ext, halo_ref, acc_ref, *out_scratch, drop_meta):
    tm = h_ref.shape[1]
    n_chunks = wg_ref.shape[0]
    fc = wg_ref.shape[2]

    @pl.when(pl.program_id(1) == 0)
    def _():
        halo_ref[...] = jnp.zeros_like(halo_ref)

    u_ref[...] = _rms_norm(h_ref[0], pre_g_ref[...]).astype(bf16)
    acc_ref[...] = jnp.zeros_like(acc_ref)

    def up(c, slot):
        u = u_ref[...]
        z_ext[slot, 0:SHORT_HALO, :] = halo_ref[c]
        z_ext[slot, SHORT_HALO:SHORT_HALO + tm, 0:fc] = jnp.dot(
            u, wg_ref[c], preferred_element_type=f32)
        z_ext[slot, SHORT_HALO:SHORT_HALO + tm, fc:2 * fc] = jnp.dot(
            u, wu_ref[c], preferred_element_type=f32)
        halo_ref[c] = z_ext[slot, tm:tm + SHORT_HALO, :]

    def down(c, slot):
        w = cw_ref[c]
        conv = jnp.zeros((tm, 2 * fc), f32)
        for t in range(FFN_KERNEL):
            off = SHORT_HALO - (FFN_KERNEL - 1) + t
            conv = conv + w[t:t + 1, :] * z_ext[slot, off:off + tm, :]
        g = conv[:, 0:fc]
        act = (g * jax.nn.sigmoid(g) * conv[:, fc:2 * fc]).astype(bf16)
        acc_ref[...] += jnp.dot(act, wd_ref[c], preferred_element_type=f32)

    assert n_chunks % 2 == 1
    up(0, 0)

    def chunk_pair(k, _):
        c = 2 * k
        up(c + 1, 1)
        down(c, 0)
        up(c + 2, 0)
        down(c + 1, 1)
        return 0

    lax.fori_loop(0, n_chunks // 2, chunk_pair, 0)
    down(n_chunks - 1, 0)
    res = h_ref[0] + _rms_norm(acc_ref[...], post_g_ref[...])
    if not drop_meta:
        o_ref[0] = res
        return

    res_ref, sem = out_scratch
    b, j = pl.program_id(0), pl.program_id(1)
    step = b * pl.num_programs(1) + j
    last_step = pl.num_programs(0) * pl.num_programs(1) - 1

    def body_copy(jj):
        return pltpu.make_async_copy(
            res_ref.at[pl.ds(N_META, tm - N_META)],
            o_ref.at[b, pl.ds(jj * tm, tm - N_META)], sem.at[0])

    def head_copy(jj):
        return pltpu.make_async_copy(
            res_ref.at[pl.ds(0, N_META)],
            o_ref.at[b, pl.ds(jj * tm - N_META, N_META)], sem.at[1])

    @pl.when(step > 0)
    def _():
        body_copy(j).wait()

    @pl.when((step > 0) & (j != 1))
    def _():
        head_copy(jnp.maximum(j, 1)).wait()

    res_ref[...] = res
    body_copy(j).start()

    @pl.when(j > 0)
    def _():
        head_copy(j).start()

    @pl.when(step == last_step)
    def _():
        body_copy(j).wait()
        head_copy(j).wait()


def _ffn(h, pre_g, wg, wu, cw, wd, post_g, drop_meta=False):
    bsz, seq, d = h.shape
    n_tiles = seq // SEQ_TILE
    n_chunks, _, fc = wg.shape
    tile = pl.BlockSpec((1, SEQ_TILE, d), lambda b, j: (b, j, 0))
    whole = lambda x: pl.BlockSpec(x.shape, lambda b, j: (0,) * x.ndim)
    params = (pre_g, wg, wu, cw, wd, post_g)
    if drop_meta:
        assert n_tiles >= 2
        out_spec = pl.BlockSpec(memory_space=pl.ANY)
        out_shape = jax.ShapeDtypeStruct((bsz, seq - N_META, d), h.dtype)
        out_scratch = [pltpu.VMEM((SEQ_TILE, d), f32), pltpu.SemaphoreType.DMA((2,))]
    else:
        out_spec, out_shape, out_scratch = tile, jax.ShapeDtypeStruct(h.shape, h.dtype), []
    return pl.pallas_call(
        functools.partial(_ffn_kernel, drop_meta=drop_meta),
        grid=(bsz, n_tiles),
        in_specs=[tile] + [whole(x) for x in params],
        out_specs=out_spec,
        out_shape=out_shape,
        scratch_shapes=[pltpu.VMEM((SEQ_TILE, d), bf16),
                        pltpu.VMEM((2, SHORT_HALO + SEQ_TILE, 2 * fc), f32),
                        pltpu.VMEM((n_chunks, SHORT_HALO, 2 * fc), f32),
                        pltpu.VMEM((SEQ_TILE, d), f32)] + out_scratch,
        compiler_params=pltpu.CompilerParams(
            dimension_semantics=("arbitrary", "arbitrary"),
            vmem_limit_bytes=VMEM_LIMIT),
        name="ffn",
    )(h, *params)


def _prep_w_in(w_in, b_forget):
    q, k, v, f, a_val, a_gate, sc_b, sc_c, sc_x = jnp.split(
        w_in, [512, 1024, 1536, 1544, 1800, 2056, 2312, 2568], axis=-1)
    pad = LANES - ATTN_HEADS * BIAS_LANES
    f_rep = jnp.pad(jnp.repeat(f, BIAS_LANES, axis=-1), ((0, 0), (0, pad)))
    w = jnp.concatenate([q, k, v, a_val, a_gate, sc_b, sc_c, sc_x, f_rep], axis=-1)
    bf = jnp.pad(jnp.repeat(b_forget, BIAS_LANES), (0, pad))[None, :]
    return w.astype(bf16), bf.astype(f32)


def _chunk_cols(w, n_chunks):
    k, n = w.shape
    return w.reshape(k, n_chunks, n // n_chunks).transpose(1, 0, 2)


def _prep_ffn(w_up, conv_w, w_down):
    n_chunks = D_FF // FF_CHUNK
    wg = _chunk_cols(w_up[:, :D_FF], n_chunks).astype(bf16)
    wu = _chunk_cols(w_up[:, D_FF:], n_chunks).astype(bf16)
    cg = _chunk_cols(conv_w[:, :D_FF], n_chunks)
    cu = _chunk_cols(conv_w[:, D_FF:], n_chunks)
    cw = jnp.concatenate([cg, cu], axis=-1)
    cw = jnp.pad(cw, ((0, 0), (0, SUBLANES - FFN_KERNEL), (0, 0)))
    wd = w_down.reshape(n_chunks, FF_CHUNK, D_MODEL).astype(bf16)
    return wg, wu, cw, wd


def kernel(x, meta_tokens, mix_pre_g, mix_post_g, w_in, b_forget, a_dw_w, a_dw_b, a_ln_g, a_ln_b, a_pw_w, sc_conv_w, head_g, w_out, ffn_pre_g, ffn_post_g, ffn_w_up, ffn_conv_w, ffn_w_down):
    depth = w_in.shape[0]
    row = lambda p: p[None, :]
    h = x
    for l in range(depth):
        w_in_r, bf_r = _prep_w_in(w_in[l], b_forget[l])
        if l == 0:
            h, q, k, v, a, cx, sb, qb, kb = _in_proj(
                x, row(mix_pre_g[l]), w_in_r, bf_r, meta=meta_tokens.astype(x.dtype))
        else:
            q, k, v, a, cx, sb, qb, kb = _in_proj(h, row(mix_pre_g[l]), w_in_r, bf_r)
        y_attn = _attention(q, k, v, qb, kb)
        h = _mixer_out(h, y_attn, a, cx, sb, a_dw_w[l], row(a_dw_b[l]), row(a_ln_g[l]),
                       row(a_ln_b[l]), a_pw_w[l].astype(bf16), sc_conv_w[l],
                       row(head_g[l]), w_out[l].astype(bf16), row(mix_post_g[l]))
        wg, wu, cw, wd = _prep_ffn(ffn_w_up[l], ffn_conv_w[l], ffn_w_down[l])
        h = _ffn(h, row(ffn_pre_g[l]), wg, wu, cw, wd, row(ffn_post_g[l]),
                 drop_meta=(l == depth - 1))
    return h
```

```python
import functools

import jax
import jax.numpy as jnp
from jax import lax
from jax.experimental import pallas as pl
from jax.experimental.pallas import tpu as pltpu

D_MODEL = 1024
N_META = 16
ATTN_HEADS = 8
HEAD_DIM = 64
D_ATTN = ATTN_HEADS * HEAD_DIM
D_CONV = 256
D_SC = 256
CONV_KERNEL = 31
SC_KERNEL = 3
D_FF = 2816
FFN_KERNEL = 3
RMS_EPS = 1e-6
LN_EPS = 1e-5
NEG_INF = -1e30
LOG2E = 1.4426950408889634

LANES = 128
SUBLANES = 8
BIAS_LANES = 6
SEQ_TILE = 688
ATTN_TILE = 256
ATTN_QUERIES = 512
SUM_ROWS = 16
CONV_HALO = 32
SHORT_HALO = 8
CONV_ROWS = 16
FF_CHUNK = 256
VMEM_LIMIT = 56 * 1024 * 1024

_OFF_Q = 0
_OFF_K = _OFF_Q + D_ATTN
_OFF_V = _OFF_K + D_ATTN
_OFF_AVAL = _OFF_V + D_ATTN
_OFF_AGATE = _OFF_AVAL + D_CONV
_OFF_SB = _OFF_AGATE + D_CONV
_OFF_SC = _OFF_SB + D_SC
_OFF_SX = _OFF_SC + D_SC
_OFF_F = _OFF_SX + D_SC
D_IN_PAD = _OFF_F + LANES

f32 = jnp.float32
bf16 = jnp.bfloat16


def _rms_norm(x, g):
    ms = jnp.mean(x * x, axis=-1, keepdims=True)
    return x * lax.rsqrt(ms + RMS_EPS) * g


def _silu(x):
    half = 0.5 * x
    return half + half * jnp.tanh(half)


def _cumsum_rows(x):
    n = x.shape[0]
    row = lax.broadcasted_iota(jnp.int32, x.shape, 0)
    shift = 1
    while shift < n:
        x = x + jnp.where(row >= shift, pltpu.roll(x, shift, axis=0), 0.0)
        shift *= 2
    return x


def _in_proj_kernel(*refs, prepend_meta):
    if prepend_meta:
        (x_ref, x_prev_ref, meta_ref, g_ref, w_ref, bf_ref, h0_ref, q_ref, k_ref, v_ref,
         a_ref, cx_ref, sb_ref, qb_ref, kb_ref, carry_ref) = refs
        tm = x_ref.shape[1]
        head = jnp.where(pl.program_id(1) == 0, meta_ref[...], x_prev_ref[0])
        h = jnp.concatenate([head, x_ref[0, 0:tm - N_META, :]], axis=0)
        h0_ref[0] = h
    else:
        (h_ref, g_ref, w_ref, bf_ref, q_ref, k_ref, v_ref,
         a_ref, cx_ref, sb_ref, qb_ref, kb_ref, carry_ref) = refs
        tm = h_ref.shape[1]
        h = h_ref[0]

    @pl.when(pl.program_id(1) == 0)
    def _():
        carry_ref[...] = jnp.zeros_like(carry_ref)

    u = _rms_norm(h, g_ref[...]).astype(bf16)

    def seg(lo, n):
        return jnp.dot(u, w_ref[:, lo:lo + n], preferred_element_type=f32)

    q_ref[0] = (seg(_OFF_Q, D_ATTN) * (HEAD_DIM ** -0.5 * LOG2E)).astype(bf16)
    k_ref[0] = seg(_OFF_K, D_ATTN).astype(bf16)
    v_ref[0] = seg(_OFF_V, D_ATTN).astype(bf16)
    a_ref[0] = (seg(_OFF_AVAL, D_CONV) * jax.nn.sigmoid(seg(_OFF_AGATE, D_CONV))).astype(bf16)
    sb_ref[0] = seg(_OFF_SB, D_SC).astype(bf16)
    cx_ref[0] = (seg(_OFF_SC, D_SC) * seg(_OFF_SX, D_SC)).astype(bf16)

    fl = seg(_OFF_F, LANES) + bf_ref[...]
    log_f = jnp.minimum(fl, 0.0) - jnp.log1p(jnp.exp(-jnp.abs(fl)))
    cum = _cumsum_rows(log_f) + carry_ref[0:1, :]
    carry_ref[...] = jnp.broadcast_to(cum[tm - 1:tm, :], carry_ref.shape)

    cum2 = cum * LOG2E
    hi = cum2.astype(bf16)
    r1 = cum2 - hi.astype(f32)
    mid = r1.astype(bf16)
    lo = (r1 - mid.astype(f32)).astype(bf16)
    lane = lax.broadcasted_iota(jnp.int32, cum.shape, 1)
    piece = lane % BIAS_LANES
    used = lane < ATTN_HEADS * BIAS_LANES
    pieces = jnp.where(piece % 3 == 0, hi, jnp.where(piece % 3 == 1, mid, lo)).astype(f32)
    qb = jnp.where(piece < 3, pieces, 1.0)
    kb = jnp.where(piece < 3, 1.0, -pieces)
    qb_ref[0] = jnp.where(used, qb, 0.0).astype(bf16)
    kb_ref[0] = jnp.where(used, kb, 0.0).astype(bf16)


def _in_proj(h, pre_g, w_in_r, bf_r, meta=None):
    prepend_meta = meta is not None
    bsz, rows, d = h.shape
    seq = rows + N_META if prepend_meta else rows
    n_tiles = seq // SEQ_TILE
    tile = lambda n: pl.BlockSpec((1, SEQ_TILE, n), lambda b, j: (b, j, 0))
    whole = lambda a: pl.BlockSpec(a.shape, lambda b, j: (0,) * a.ndim)
    out = lambda n: jax.ShapeDtypeStruct((bsz, seq, n), bf16)
    in_specs = [tile(d), whole(pre_g), whole(w_in_r), whole(bf_r)]
    operands = [h, pre_g, w_in_r, bf_r]
    out_specs = [tile(D_ATTN), tile(D_ATTN), tile(D_ATTN), tile(D_CONV),
                 tile(D_SC), tile(D_SC), tile(LANES), tile(LANES)]
    out_shape = [out(D_ATTN), out(D_ATTN), out(D_ATTN), out(D_CONV),
                 out(D_SC), out(D_SC), out(LANES), out(LANES)]
    if prepend_meta:
        per_tile = SEQ_TILE // N_META
        prev_rows = pl.BlockSpec((1, N_META, d),
                                 lambda b, j: (b, jnp.maximum(j * per_tile - 1, 0), 0))
        in_specs = [tile(d), prev_rows, whole(meta)] + in_specs[1:]
        operands = [h, h, meta] + operands[1:]
        out_specs = [tile(d)] + out_specs
        out_shape = [jax.ShapeDtypeStruct((bsz, seq, d), h.dtype)] + out_shape
    return pl.pallas_call(
        functools.partial(_in_proj_kernel, prepend_meta=prepend_meta),
        grid=(bsz, n_tiles),
        in_specs=in_specs,
        out_specs=out_specs,
        out_shape=out_shape,
        scratch_shapes=[pltpu.VMEM((SUBLANES, LANES), f32)],
        compiler_params=pltpu.CompilerParams(
            dimension_semantics=("arbitrary", "arbitrary"),
            vmem_limit_bytes=VMEM_LIMIT),
        name="in_proj",
    )(*operands)


def _attn_kernel(q_ref, k_ref, v_ref, qb_ref, kb_ref, o_ref, vt_ref, vtm_ref,
                 qt_ref, st_ref, sf_ref):
    seq = q_ref.shape[1]
    n_tiles = (seq - N_META) // ATTN_TILE
    pair = pl.program_id(1)
    lane = lax.broadcasted_iota(jnp.int32, (1, LANES), 1)
    nt_dims = (((1,), (1,)), ((), ()))

    q_masks, b_masks = [], []
    for hh in range(2):
        head = 2 * pair + hh
        q_masks.append((lane // HEAD_DIM == hh).astype(bf16))
        b_masks.append(((lane >= BIAS_LANES * head) & (lane < BIAS_LANES * (head + 1))).astype(bf16))

    def queries(start, size, hh):
        return jnp.concatenate([q_ref[0, pl.ds(start, size), :] * q_masks[hh],
                                qb_ref[0, pl.ds(start, size), :] * b_masks[hh]], axis=1)

    def keys(start, size):
        return jnp.concatenate([k_ref[0, pl.ds(start, size), :],
                                kb_ref[0, pl.ds(start, size), :]], axis=1)

    def with_ones(vt_pair):
        ones = jnp.ones((SUM_ROWS, vt_pair.shape[1]), bf16)
        return [jnp.concatenate([vt_pair[hh * HEAD_DIM:(hh + 1) * HEAD_DIM], ones], axis=0)
                for hh in range(2)]

    v_meta = jnp.concatenate([v_ref[0, 0:N_META, :].astype(f32),
                              jnp.zeros((LANES - N_META, LANES), f32)], axis=0)
    for hh, vt_h in enumerate(with_ones(v_meta.T.astype(bf16))):
        vtm_ref[hh] = vt_h
    for j in range(n_tiles):
        lo = N_META + j * ATTN_TILE
        vt_pair = v_ref[0, lo:lo + ATTN_TILE, :].astype(f32).T.astype(bf16)
        for hh, vt_h in enumerate(with_ones(vt_pair)):
            vt_ref[j, hh] = vt_h

    lo_lanes = lax.broadcasted_iota(jnp.int32, (N_META, LANES), 1) < HEAD_DIM
    k_meta = keys(0, N_META)
    o_meta = []
    for hh in range(2):
        s = lax.dot_general(queries(0, N_META, hh), k_meta, nt_dims, preferred_element_type=f32)
        r = lax.broadcasted_iota(jnp.int32, s.shape, 0)
        c = lax.broadcasted_iota(jnp.int32, s.shape, 1)
        s = jnp.where(r >= c, s, NEG_INF)
        p = jnp.exp2(s - jnp.max(s, axis=-1, keepdims=True))
        o = jnp.dot(p.astype(bf16), v_ref[0, 0:N_META, :], preferred_element_type=f32)
        o_meta.append(o * (1.0 / jnp.sum(p, axis=-1, keepdims=True)))
    o_ref[0, 0:N_META, :] = jnp.where(lo_lanes, o_meta[0], o_meta[1]).astype(bf16)

    first = N_META + ATTN_TILE
    n_q = (seq - N_META) // ATTN_QUERIES
    k_per_q = ATTN_QUERIES // ATTN_TILE

    def put_queries(i, slot):
        start = N_META + i * ATTN_QUERIES
        for hh in range(2):
            qt_ref[slot, hh] = queries(start, ATTN_QUERIES, hh).astype(f32).T.astype(bf16)

    def put_scores(dst, start, size, q_slot):
        kj = keys(start, size)
        for hh in range(2):
            dst[hh] = jnp.dot(kj, qt_ref[q_slot, hh], preferred_element_type=f32)

    def put_tile_scores(j, q_slot):
        put_scores(st_ref.at[j % 2], N_META + j * ATTN_TILE, ATTN_TILE, q_slot)

    def first_block(hh, mask):
        st = sf_ref[hh]
        if mask is not None:
            st = jnp.where(mask, st, NEG_INF)
        m = jnp.max(st, axis=0, keepdims=True)
        pt = jnp.exp2(st - m).astype(bf16)
        pt_meta = jnp.concatenate(
            [pt[0:N_META], jnp.zeros((LANES - N_META, ATTN_QUERIES), bf16)], axis=0)
        acc = (jnp.dot(vt_ref[0, hh], pt[N_META:first], preferred_element_type=f32)
               + jnp.dot(vtm_ref[hh], pt_meta, preferred_element_type=f32))
        return m, acc

    def update(carry, j, hh, mask):
        m, acc = carry
        st = st_ref[j % 2, hh]
        if mask is not None:
            st = jnp.where(mask, st, NEG_INF)
        m_new = jnp.maximum(m, jnp.max(st, axis=0, keepdims=True))
        alpha = jnp.exp2(m - m_new)
        pt = jnp.exp2(st - m_new).astype(bf16)
        acc = alpha * acc + jnp.dot(vt_ref[j, hh], pt, preferred_element_type=f32)
        return m_new, acc

    def store(i, carries):
        o_t = jnp.concatenate(
            [acc[0:HEAD_DIM] * (1.0 / acc[HEAD_DIM:HEAD_DIM + 1]) for _, acc in carries], axis=0)
        start = N_META + i * ATTN_QUERIES
        o_ref[0, start:start + ATTN_QUERIES, :] = o_t.T.astype(bf16)

    kk = lax.broadcasted_iota(jnp.int32, (ATTN_TILE, ATTN_QUERIES), 0)
    qq = lax.broadcasted_iota(jnp.int32, (ATTN_TILE, ATTN_QUERIES), 1)
    kk_first = lax.broadcasted_iota(jnp.int32, (first, ATTN_QUERIES), 0)
    qq_first = lax.broadcasted_iota(jnp.int32, (first, ATTN_QUERIES), 1)

    put_queries(0, 0)
    put_scores(sf_ref, 0, first, 0)
    for i in range(n_q):
        q_slot = i % 2
        n_full = i * k_per_q
        put_tile_scores(1, q_slot)
        if i == 0:
            carries = [first_block(hh, qq_first >= kk_first - N_META) for hh in range(2)]
            diag = range(1, k_per_q)
        else:
            carries = [first_block(hh, None) for hh in range(2)]
            for j in range(1, n_full):
                put_tile_scores(j + 1, q_slot)
                carries = [update(carries[hh], j, hh, None) for hh in range(2)]
            diag = range(k_per_q)
        for t in diag:
            j = n_full + t
            if t + 1 < k_per_q:
                put_tile_scores(j + 1, q_slot)
            elif i + 1 < n_q:
                put_queries(i + 1, 1 - q_slot)
                put_scores(sf_ref, 0, first, 1 - q_slot)
            mask = qq >= kk + t * ATTN_TILE
            carries = [update(carries[hh], j, hh, mask) for hh in range(2)]
        store(i, carries)


def _attention(q, k, v, qb, kb):
    bsz, seq, _ = q.shape
    n_tiles = (seq - N_META) // ATTN_TILE
    pair_blk = pl.BlockSpec((1, seq, LANES), lambda b, p: (b, 0, p))
    bias_blk = pl.BlockSpec((1, seq, LANES), lambda b, p: (b, 0, 0))
    return pl.pallas_call(
        _attn_kernel,
        grid=(bsz, D_ATTN // LANES),
        in_specs=[pair_blk, pair_blk, pair_blk, bias_blk, bias_blk],
        out_specs=pair_blk,
        out_shape=jax.ShapeDtypeStruct((bsz, seq, D_ATTN), bf16),
        scratch_shapes=[pltpu.VMEM((n_tiles, 2, HEAD_DIM + SUM_ROWS, ATTN_TILE), bf16),
                        pltpu.VMEM((2, HEAD_DIM + SUM_ROWS, LANES), bf16),
                        pltpu.VMEM((2, 2, 2 * LANES, ATTN_QUERIES), bf16),
                        pltpu.VMEM((2, 2, ATTN_TILE, ATTN_QUERIES), f32),
                        pltpu.VMEM((2, N_META + ATTN_TILE, ATTN_QUERIES), f32)],
        compiler_params=pltpu.CompilerParams(
            dimension_semantics=("arbitrary", "arbitrary"),
            vmem_limit_bytes=VMEM_LIMIT),
        name="attention",
    )(q, k, v, qb, kb)


def _head_norm_block(y, g):
    lo_lanes = lax.broadcasted_iota(jnp.int32, y.shape, 1) < HEAD_DIM
    y2 = y * y
    s_lo = jnp.sum(jnp.where(lo_lanes, y2, 0.0), axis=-1, keepdims=True)
    s_hi = jnp.sum(jnp.where(lo_lanes, 0.0, y2), axis=-1, keepdims=True)
    r_lo = lax.rsqrt(s_lo * (1.0 / HEAD_DIM) + RMS_EPS)
    r_hi = lax.rsqrt(s_hi * (1.0 / HEAD_DIM) + RMS_EPS)
    return y * jnp.where(lo_lanes, r_lo, r_hi) * g


def _mixer_out_kernel(h_ref, ya_ref, a_ref, cx_ref, sb_ref, dw_w_ref, dw_b_ref,
                      ln_g_ref, ln_b_ref, pw_ref, scw_ref, hg_ref, wo_ref, pg_ref,
                      o_ref, a_ext, cx_ext, a_sh, conv_ref, w_bc):
    tm = h_ref.shape[1]

    @pl.when(pl.program_id(1) == 0)
    def _():
        a_ext[0:CONV_HALO, :] = jnp.zeros((CONV_HALO, D_CONV), f32)
        cx_ext[0:SHORT_HALO, :] = jnp.zeros((SHORT_HALO, D_SC), f32)

    @pl.when(pl.program_id(1) > 0)
    def _():
        a_ext[0:CONV_HALO, :] = a_ext[tm:tm + CONV_HALO, :]
        cx_ext[0:SHORT_HALO, :] = cx_ext[tm:tm + SHORT_HALO, :]

    a_ext[CONV_HALO:CONV_HALO + tm, :] = a_ref[0].astype(f32)
    cx_ext[SHORT_HALO:SHORT_HALO + tm, :] = cx_ref[0].astype(f32)

    span = tm + CONV_HALO - SUBLANES
    for r in range(1, SUBLANES):
        a_sh[r - 1] = a_ext[r:r + span, :]
    for t in range(CONV_KERNEL):
        w_bc[t] = jnp.broadcast_to(dw_w_ref[t:t + 1, :], (SUBLANES, D_CONV))

    def conv_rows(rb, _):
        base = pl.multiple_of(rb * CONV_ROWS, CONV_ROWS)
        acc = jnp.broadcast_to(dw_b_ref[...], (CONV_ROWS, D_CONV))
        first_off = CONV_HALO - (CONV_KERNEL - 1)
        for r in range(SUBLANES):
            ms = [off // SUBLANES for off in range(first_off, CONV_HALO + 1)
                  if off % SUBLANES == r]
            rows = pl.ds(pl.multiple_of(base + ms[0] * SUBLANES, SUBLANES),
                         (ms[-1] - ms[0]) * SUBLANES + CONV_ROWS)
            window = a_ext[rows, :] if r == 0 else a_sh[r - 1, rows, :]
            for m in ms:
                t = m * SUBLANES + r - first_off
                lo = (m - ms[0]) * SUBLANES
                w = jnp.concatenate([w_bc[t]] * (CONV_ROWS // SUBLANES), axis=0)
                acc = acc + w * window[lo:lo + CONV_ROWS, :]
        conv_ref[pl.ds(base, CONV_ROWS), :] = acc
        return 0

    lax.fori_loop(0, tm // CONV_ROWS, conv_rows, 0)
    conv = conv_ref[...]
    mu = jnp.mean(conv, axis=-1, keepdims=True)
    cen = conv - mu
    var = jnp.mean(cen * cen, axis=-1, keepdims=True)
    ln = cen * lax.rsqrt(var + LN_EPS) * ln_g_ref[...] + ln_b_ref[...]
    y_conv = jnp.dot(_silu(ln).astype(bf16), pw_ref[...], preferred_element_type=f32)

    sc = None
    for t in range(SC_KERNEL):
        off = SHORT_HALO - (SC_KERNEL - 1) + t
        term = scw_ref[t:t + 1, :] * cx_ext[off:off + tm, :]
        sc = term if sc is None else sc + term
    y_sc = sb_ref[0].astype(f32) * sc

    blocks = []
    for c in range(D_MODEL // LANES):
        lo = c * LANES
        if lo < D_ATTN:
            yb = ya_ref[0, :, lo:lo + LANES].astype(f32)
        elif lo < D_ATTN + D_CONV:
            yb = y_conv[:, lo - D_ATTN:lo - D_ATTN + LANES]
        else:
            yb = y_sc[:, lo - D_ATTN - D_CONV:lo - D_ATTN - D_CONV + LANES]
        blocks.append(_head_norm_block(yb, hg_ref[:, lo:lo + LANES]).astype(bf16))
    yn = jnp.concatenate(blocks, axis=-1)
    y = jnp.dot(yn, wo_ref[...], preferred_element_type=f32)
    o_ref[0] = h_ref[0] + _rms_norm(y, pg_ref[...])


def _mixer_out(h, y_attn, a, cx, sb, dw_w, dw_b, ln_g, ln_b, pw, scw, head_g, w_out, post_g):
    bsz, seq, d = h.shape
    n_tiles = seq // SEQ_TILE
    tile = lambda n: pl.BlockSpec((1, SEQ_TILE, n), lambda b, j: (b, j, 0))
    whole = lambda x: pl.BlockSpec(x.shape, lambda b, j: (0,) * x.ndim)
    params = (dw_w, dw_b, ln_g, ln_b, pw, scw, head_g, w_out, post_g)
    return pl.pallas_call(
        _mixer_out_kernel,
        grid=(bsz, n_tiles),
        in_specs=[tile(d), tile(D_ATTN), tile(D_CONV), tile(D_SC), tile(D_SC)]
                 + [whole(x) for x in params],
        out_specs=tile(d),
        out_shape=jax.ShapeDtypeStruct(h.shape, h.dtype),
        scratch_shapes=[pltpu.VMEM((CONV_HALO + SEQ_TILE, D_CONV), f32),
                        pltpu.VMEM((SHORT_HALO + SEQ_TILE, D_SC), f32),
                        pltpu.VMEM((SUBLANES - 1, SEQ_TILE + CONV_HALO - SUBLANES, D_CONV), f32),
                        pltpu.VMEM((SEQ_TILE, D_CONV), f32),
                        pltpu.VMEM((CONV_KERNEL, SUBLANES, D_CONV), f32)],
        compiler_params=pltpu.CompilerParams(
            dimension_semantics=("arbitrary", "arbitrary"),
            vmem_limit_bytes=VMEM_LIMIT),
        name="mixer_out",
    )(h, y_attn, a, cx, sb, *params)


def _ffn_kernel(h_ref, pre_g_ref, wg_ref, wu_ref, cw_ref, wd_ref, post_g_ref,
                o_ref, u_ref, z_ext, halo_ref, acc_ref, *out_scratch, drop_meta):
    tm = h_ref.shape[1]
    n_chunks = wg_ref.shape[0]
    fc = wg_ref.shape[2]

    @pl.when(pl.program_id(1) == 0)
    def _():
        halo_ref[...] = jnp.zeros_like(halo_ref)

    u_ref[...] = _rms_norm(h_ref[0], pre_g_ref[...]).astype(bf16)
    acc_ref[...] = jnp.zeros_like(acc_ref)

    def up(c, slot):
        u = u_ref[...]
        z_ext[slot, 0:SHORT_HALO, :] = halo_ref[c]
        z_ext[slot, SHORT_HALO:SHORT_HALO + tm, 0:fc] = jnp.dot(
            u, wg_ref[c], preferred_element_type=f32)
        z_ext[slot, SHORT_HALO:SHORT_HALO + tm, fc:2 * fc] = jnp.dot(
            u, wu_ref[c], preferred_element_type=f32)
        halo_ref[c] = z_ext[slot, tm:tm + SHORT_HALO, :]

    def down(c, slot):
        w = cw_ref[c]
        conv = None
        for t in range(FFN_KERNEL):
            off = SHORT_HALO - (FFN_KERNEL - 1) + t
            term = w[t:t + 1, :] * z_ext[slot, off:off + tm, :]
            conv = term if conv is None else conv + term
        act = (_silu(conv[:, 0:fc]) * conv[:, fc:2 * fc]).astype(bf16)
        acc_ref[...] += jnp.dot(act, wd_ref[c], preferred_element_type=f32)

    assert n_chunks % 2 == 1
    up(0, 0)

    def chunk_pair(k, _):
        c = 2 * k
        up(c + 1, 1)
        down(c, 0)
        up(c + 2, 0)
        down(c + 1, 1)
        return 0

    lax.fori_loop(0, n_chunks // 2, chunk_pair, 0)
    down(n_chunks - 1, 0)
    res = h_ref[0] + _rms_norm(acc_ref[...], post_g_ref[...])
    if not drop_meta:
        o_ref[0] = res
        return

    res_ref, sem = out_scratch
    b, j = pl.program_id(0), pl.program_id(1)
    step = b * pl.num_programs(1) + j
    last_step = pl.num_programs(0) * pl.num_programs(1) - 1

    def body_copy(jj):
        return pltpu.make_async_copy(
            res_ref.at[pl.ds(N_META, tm - N_META)],
            o_ref.at[b, pl.ds(jj * tm, tm - N_META)], sem.at[0])

    def head_copy(jj):
        return pltpu.make_async_copy(
            res_ref.at[pl.ds(0, N_META)],
            o_ref.at[b, pl.ds(jj * tm - N_META, N_META)], sem.at[1])

    @pl.when(step > 0)
    def _():
        body_copy(j).wait()

    @pl.when((step > 0) & (j != 1))
    def _():
        head_copy(jnp.maximum(j, 1)).wait()

    res_ref[...] = res
    body_copy(j).start()

    @pl.when(j > 0)
    def _():
        head_copy(j).start()

    @pl.when(step == last_step)
    def _():
        body_copy(j).wait()
        head_copy(j).wait()


def _ffn(h, pre_g, wg, wu, cw, wd, post_g, drop_meta=False):
    bsz, seq, d = h.shape
    n_tiles = seq // SEQ_TILE
    n_chunks, _, fc = wg.shape
    tile = pl.BlockSpec((1, SEQ_TILE, d), lambda b, j: (b, j, 0))
    whole = lambda x: pl.BlockSpec(x.shape, lambda b, j: (0,) * x.ndim)
    params = (pre_g, wg, wu, cw, wd, post_g)
    if drop_meta:
        assert n_tiles >= 2
        out_spec = pl.BlockSpec(memory_space=pl.ANY)
        out_shape = jax.ShapeDtypeStruct((bsz, seq - N_META, d), h.dtype)
        out_scratch = [pltpu.VMEM((SEQ_TILE, d), f32), pltpu.SemaphoreType.DMA((2,))]
    else:
        out_spec, out_shape, out_scratch = tile, jax.ShapeDtypeStruct(h.shape, h.dtype), []
    return pl.pallas_call(
        functools.partial(_ffn_kernel, drop_meta=drop_meta),
        grid=(bsz, n_tiles),
        in_specs=[tile] + [whole(x) for x in params],
        out_specs=out_spec,
        out_shape=out_shape,
        scratch_shapes=[pltpu.VMEM((SEQ_TILE, d), bf16),
                        pltpu.VMEM((2, SHORT_HALO + SEQ_TILE, 2 * fc), f32),
                        pltpu.VMEM((n_chunks, SHORT_HALO, 2 * fc), f32),
                        pltpu.VMEM((SEQ_TILE, d), f32)] + out_scratch,
        compiler_params=pltpu.CompilerParams(
            dimension_semantics=("arbitrary", "arbitrary"),
            vmem_limit_bytes=VMEM_LIMIT),
        name="ffn",
    )(h, *params)


def _prep_w_in(w_in, b_forget):
    q, k, v, f, a_val, a_gate, sc_b, sc_c, sc_x = jnp.split(
        w_in, [512, 1024, 1536, 1544, 1800, 2056, 2312, 2568], axis=-1)
    pad = LANES - ATTN_HEADS * BIAS_LANES
    f_rep = jnp.pad(jnp.repeat(f, BIAS_LANES, axis=-1), ((0, 0), (0, pad)))
    w = jnp.concatenate([q, k, v, a_val, a_gate, sc_b, sc_c, sc_x, f_rep], axis=-1)
    bf = jnp.pad(jnp.repeat(b_forget, BIAS_LANES), (0, pad))[None, :]
    return w.astype(bf16), bf.astype(f32)


def _chunk_cols(w, n_chunks):
    k, n = w.shape
    return w.reshape(k, n_chunks, n // n_chunks).transpose(1, 0, 2)


def _prep_ffn(w_up, conv_w, w_down):
    n_chunks = D_FF // FF_CHUNK
    wg = _chunk_cols(w_up[:, :D_FF], n_chunks).astype(bf16)
    wu = _chunk_cols(w_up[:, D_FF:], n_chunks).astype(bf16)
    cg = _chunk_cols(conv_w[:, :D_FF], n_chunks)
    cu = _chunk_cols(conv_w[:, D_FF:], n_chunks)
    cw = jnp.concatenate([cg, cu], axis=-1)
    cw = jnp.pad(cw, ((0, 0), (0, SUBLANES - FFN_KERNEL), (0, 0)))
    wd = w_down.reshape(n_chunks, FF_CHUNK, D_MODEL).astype(bf16)
    return wg, wu, cw, wd


def kernel(x, meta_tokens, mix_pre_g, mix_post_g, w_in, b_forget, a_dw_w, a_dw_b, a_ln_g, a_ln_b, a_pw_w, sc_conv_w, head_g, w_out, ffn_pre_g, ffn_post_g, ffn_w_up, ffn_conv_w, ffn_w_down):
    depth = w_in.shape[0]
    row = lambda p: p[None, :]
    h = x
    for l in range(depth):
        w_in_r, bf_r = _prep_w_in(w_in[l], b_forget[l])
        if l == 0:
            h, q, k, v, a, cx, sb, qb, kb = _in_proj(
                x, row(mix_pre_g[l]), w_in_r, bf_r, meta=meta_tokens.astype(x.dtype))
        else:
            q, k, v, a, cx, sb, qb, kb = _in_proj(h, row(mix_pre_g[l]), w_in_r, bf_r)
        y_attn = _attention(q, k, v, qb, kb)
        h = _mixer_out(h, y_attn, a, cx, sb, a_dw_w[l], row(a_dw_b[l]), row(a_ln_g[l]),
                       row(a_ln_b[l]), a_pw_w[l].astype(bf16), sc_conv_w[l],
                       row(head_g[l]), w_out[l].astype(bf16), row(mix_post_g[l]))
        wg, wu, cw, wd = _prep_ffn(ffn_w_up[l], ffn_conv_w[l], ffn_w_down[l])
        h = _ffn(h, row(ffn_pre_g[l]), wg, wu, cw, wd, row(ffn_post_g[l]),
                 drop_meta=(l == depth - 1))
    return h
```

```python
import functools

import jax
import jax.numpy as jnp
from jax import lax
from jax.experimental import pallas as pl
from jax.experimental.pallas import tpu as pltpu

D_MODEL = 1024
N_META = 16
ATTN_HEADS = 8
HEAD_DIM = 64
D_ATTN = ATTN_HEADS * HEAD_DIM
D_CONV = 256
D_SC = 256
CONV_KERNEL = 31
SC_KERNEL = 3
D_FF = 2816
FFN_KERNEL = 3
RMS_EPS = 1e-6
LN_EPS = 1e-5
NEG_INF = -1e30
LOG2E = 1.4426950408889634

LANES = 128
SUBLANES = 8
BIAS_LANES = 6
SEQ_TILE = 688
ATTN_TILE = 256
ATTN_QUERIES = 512
SUM_ROWS = 16
CONV_HALO = 32
SHORT_HALO = 8
CONV_ROWS = 16
FF_CHUNK = 256
VMEM_LIMIT = 56 * 1024 * 1024

_OFF_Q = 0
_OFF_K = _OFF_Q + D_ATTN
_OFF_V = _OFF_K + D_ATTN
_OFF_AVAL = _OFF_V + D_ATTN
_OFF_AGATE = _OFF_AVAL + D_CONV
_OFF_SB = _OFF_AGATE + D_CONV
_OFF_SC = _OFF_SB + D_SC
_OFF_SX = _OFF_SC + D_SC
_OFF_F = _OFF_SX + D_SC
D_IN_PAD = _OFF_F + LANES

f32 = jnp.float32
bf16 = jnp.bfloat16


def _rms_norm(x, g):
    ms = jnp.mean(x * x, axis=-1, keepdims=True)
    return x * lax.rsqrt(ms + RMS_EPS) * g


def _silu(x):
    half = 0.5 * x
    return half + half * jnp.tanh(half)


def _cumsum_rows(x):
    n = x.shape[0]
    row = lax.broadcasted_iota(jnp.int32, x.shape, 0)
    shift = 1
    while shift < n:
        x = x + jnp.where(row >= shift, pltpu.roll(x, shift, axis=0), 0.0)
        shift *= 2
    return x


def _in_proj_kernel(*refs, prepend_meta):
    if prepend_meta:
        (x_ref, x_prev_ref, meta_ref, g_ref, w_ref, bf_ref, h0_ref, q_ref, k_ref, v_ref,
         a_ref, cx_ref, sb_ref, qb_ref, kb_ref, carry_ref) = refs
        tm = x_ref.shape[1]
        head = jnp.where(pl.program_id(1) == 0, meta_ref[...], x_prev_ref[0])
        h = jnp.concatenate([head, x_ref[0, 0:tm - N_META, :]], axis=0)
        h0_ref[0] = h
    else:
        (h_ref, g_ref, w_ref, bf_ref, q_ref, k_ref, v_ref,
         a_ref, cx_ref, sb_ref, qb_ref, kb_ref, carry_ref) = refs
        tm = h_ref.shape[1]
        h = h_ref[0]

    @pl.when(pl.program_id(1) == 0)
    def _():
        carry_ref[...] = jnp.zeros_like(carry_ref)

    u = _rms_norm(h, g_ref[...]).astype(bf16)

    def seg(lo, n):
        return jnp.dot(u, w_ref[:, lo:lo + n], preferred_element_type=f32)

    q_ref[0] = (seg(_OFF_Q, D_ATTN) * (HEAD_DIM ** -0.5 * LOG2E)).astype(bf16)
    k_ref[0] = seg(_OFF_K, D_ATTN).astype(bf16)
    v_ref[0] = seg(_OFF_V, D_ATTN).astype(bf16)
    a_ref[0] = (seg(_OFF_AVAL, D_CONV) * jax.nn.sigmoid(seg(_OFF_AGATE, D_CONV))).astype(bf16)
    sb_ref[0] = seg(_OFF_SB, D_SC).astype(bf16)
    cx_ref[0] = (seg(_OFF_SC, D_SC) * seg(_OFF_SX, D_SC)).astype(bf16)

    fl = seg(_OFF_F, LANES) + bf_ref[...]
    log_f = jnp.minimum(fl, 0.0) - jnp.log1p(jnp.exp(-jnp.abs(fl)))
    cum = _cumsum_rows(log_f) + carry_ref[0:1, :]
    carry_ref[...] = jnp.broadcast_to(cum[tm - 1:tm, :], carry_ref.shape)

    cum2 = cum * LOG2E
    hi = cum2.astype(bf16)
    r1 = cum2 - hi.astype(f32)
    mid = r1.astype(bf16)
    lo = (r1 - mid.astype(f32)).astype(bf16)
    lane = lax.broadcasted_iota(jnp.int32, cum.shape, 1)
    piece = lane % BIAS_LANES
    used = lane < ATTN_HEADS * BIAS_LANES
    pieces = jnp.where(piece % 3 == 0, hi, jnp.where(piece % 3 == 1, mid, lo)).astype(f32)
    qb = jnp.where(piece < 3, pieces, 1.0)
    kb = jnp.where(piece < 3, 1.0, -pieces)
    qb_ref[0] = jnp.where(used, qb, 0.0).astype(bf16)
    kb_ref[0] = jnp.where(used, kb, 0.0).astype(bf16)


def _in_proj(h, pre_g, w_in_r, bf_r, meta=None):
    prepend_meta = meta is not None
    bsz, rows, d = h.shape
    seq = rows + N_META if prepend_meta else rows
    assert seq % SEQ_TILE == 0 and SEQ_TILE % N_META == 0
    n_tiles = seq // SEQ_TILE
    tile = lambda n: pl.BlockSpec((1, SEQ_TILE, n), lambda b, j: (b, j, 0))
    whole = lambda a: pl.BlockSpec(a.shape, lambda b, j: (0,) * a.ndim)
    out = lambda n: jax.ShapeDtypeStruct((bsz, seq, n), bf16)
    in_specs = [tile(d), whole(pre_g), whole(w_in_r), whole(bf_r)]
    operands = [h, pre_g, w_in_r, bf_r]
    out_specs = [tile(D_ATTN), tile(D_ATTN), tile(D_ATTN), tile(D_CONV),
                 tile(D_SC), tile(D_SC), tile(LANES), tile(LANES)]
    out_shape = [out(D_ATTN), out(D_ATTN), out(D_ATTN), out(D_CONV),
                 out(D_SC), out(D_SC), out(LANES), out(LANES)]
    if prepend_meta:
        per_tile = SEQ_TILE // N_META
        prev_rows = pl.BlockSpec((1, N_META, d),
                                 lambda b, j: (b, jnp.maximum(j * per_tile - 1, 0), 0))
        in_specs = [tile(d), prev_rows, whole(meta)] + in_specs[1:]
        operands = [h, h, meta] + operands[1:]
        out_specs = [tile(d)] + out_specs
        out_shape = [jax.ShapeDtypeStruct((bsz, seq, d), h.dtype)] + out_shape
    return pl.pallas_call(
        functools.partial(_in_proj_kernel, prepend_meta=prepend_meta),
        grid=(bsz, n_tiles),
        in_specs=in_specs,
        out_specs=out_specs,
        out_shape=out_shape,
        scratch_shapes=[pltpu.VMEM((SUBLANES, LANES), f32)],
        compiler_params=pltpu.CompilerParams(
            dimension_semantics=("arbitrary", "arbitrary"),
            vmem_limit_bytes=VMEM_LIMIT),
        name="in_proj",
    )(*operands)


def _attn_kernel(q_ref, k_ref, v_ref, qb_ref, kb_ref, o_ref, vt_ref, vtm_ref,
                 qt_ref, st_ref, sf_ref):
    seq = q_ref.shape[1]
    n_tiles = (seq - N_META) // ATTN_TILE
    pair = pl.program_id(1)
    lane = lax.broadcasted_iota(jnp.int32, (1, LANES), 1)
    nt_dims = (((1,), (1,)), ((), ()))

    q_masks, b_masks = [], []
    for hh in range(2):
        head = 2 * pair + hh
        q_masks.append((lane // HEAD_DIM == hh).astype(bf16))
        b_masks.append(((lane >= BIAS_LANES * head) & (lane < BIAS_LANES * (head + 1))).astype(bf16))

    def queries(start, size, hh):
        return jnp.concatenate([q_ref[0, pl.ds(start, size), :] * q_masks[hh],
                                qb_ref[0, pl.ds(start, size), :] * b_masks[hh]], axis=1)

    def keys(start, size):
        return jnp.concatenate([k_ref[0, pl.ds(start, size), :],
                                kb_ref[0, pl.ds(start, size), :]], axis=1)

    def with_ones(vt_pair):
        ones = jnp.ones((SUM_ROWS, vt_pair.shape[1]), bf16)
        return [jnp.concatenate([vt_pair[hh * HEAD_DIM:(hh + 1) * HEAD_DIM], ones], axis=0)
                for hh in range(2)]

    v_meta = jnp.concatenate([v_ref[0, 0:N_META, :].astype(f32),
                              jnp.zeros((LANES - N_META, LANES), f32)], axis=0)
    for hh, vt_h in enumerate(with_ones(v_meta.T.astype(bf16))):
        vtm_ref[hh] = vt_h
    for j in range(n_tiles):
        lo = N_META + j * ATTN_TILE
        vt_pair = v_ref[0, lo:lo + ATTN_TILE, :].astype(f32).T.astype(bf16)
        for hh, vt_h in enumerate(with_ones(vt_pair)):
            vt_ref[j, hh] = vt_h

    lo_lanes = lax.broadcasted_iota(jnp.int32, (N_META, LANES), 1) < HEAD_DIM
    k_meta = keys(0, N_META)
    o_meta = []
    for hh in range(2):
        s = lax.dot_general(queries(0, N_META, hh), k_meta, nt_dims, preferred_element_type=f32)
        r = lax.broadcasted_iota(jnp.int32, s.shape, 0)
        c = lax.broadcasted_iota(jnp.int32, s.shape, 1)
        s = jnp.where(r >= c, s, NEG_INF)
        p = jnp.exp2(s - jnp.max(s, axis=-1, keepdims=True))
        o = jnp.dot(p.astype(bf16), v_ref[0, 0:N_META, :], preferred_element_type=f32)
        o_meta.append(o * (1.0 / jnp.sum(p, axis=-1, keepdims=True)))
    o_ref[0, 0:N_META, :] = jnp.where(lo_lanes, o_meta[0], o_meta[1]).astype(bf16)

    first = N_META + ATTN_TILE
    n_q = (seq - N_META) // ATTN_QUERIES
    k_per_q = ATTN_QUERIES // ATTN_TILE
    assert k_per_q >= 2 and ATTN_QUERIES % ATTN_TILE == 0 and (seq - N_META) % ATTN_QUERIES == 0

    def put_queries(i, slot):
        start = N_META + i * ATTN_QUERIES
        for hh in range(2):
            qt_ref[slot, hh] = queries(start, ATTN_QUERIES, hh).astype(f32).T.astype(bf16)

    def put_scores(dst, start, size, q_slot):
        kj = keys(start, size)
        for hh in range(2):
            dst[hh] = jnp.dot(kj, qt_ref[q_slot, hh], preferred_element_type=f32)

    def put_tile_scores(j, q_slot):
        put_scores(st_ref.at[j % 2], N_META + j * ATTN_TILE, ATTN_TILE, q_slot)

    def first_block(hh, mask):
        st = sf_ref[hh]
        if mask is not None:
            st = jnp.where(mask, st, NEG_INF)
        m = jnp.max(st, axis=0, keepdims=True)
        pt = jnp.exp2(st - m).astype(bf16)
        pt_meta = jnp.concatenate(
            [pt[0:N_META], jnp.zeros((LANES - N_META, ATTN_QUERIES), bf16)], axis=0)
        acc = (jnp.dot(vt_ref[0, hh], pt[N_META:first], preferred_element_type=f32)
               + jnp.dot(vtm_ref[hh], pt_meta, preferred_element_type=f32))
        return m, acc

    def update(carry, j, hh, mask):
        m, acc = carry
        st = st_ref[j % 2, hh]
        if mask is not None:
            st = jnp.where(mask, st, NEG_INF)
        m_new = jnp.maximum(m, jnp.max(st, axis=0, keepdims=True))
        alpha = jnp.exp2(m - m_new)
        pt = jnp.exp2(st - m_new).astype(bf16)
        acc = alpha * acc + jnp.dot(vt_ref[j, hh], pt, preferred_element_type=f32)
        return m_new, acc

    def store(i, carries):
        o_t = jnp.concatenate(
            [acc[0:HEAD_DIM] * (1.0 / acc[HEAD_DIM:HEAD_DIM + 1]) for _, acc in carries], axis=0)
        start = N_META + i * ATTN_QUERIES
        o_ref[0, start:start + ATTN_QUERIES, :] = o_t.T.astype(bf16)

    kk = lax.broadcasted_iota(jnp.int32, (ATTN_TILE, ATTN_QUERIES), 0)
    qq = lax.broadcasted_iota(jnp.int32, (ATTN_TILE, ATTN_QUERIES), 1)
    kk_first = lax.broadcasted_iota(jnp.int32, (first, ATTN_QUERIES), 0)
    qq_first = lax.broadcasted_iota(jnp.int32, (first, ATTN_QUERIES), 1)

    put_queries(0, 0)
    put_scores(sf_ref, 0, first, 0)
    for i in range(n_q):
        q_slot = i % 2
        n_full = i * k_per_q
        put_tile_scores(1, q_slot)
        if i == 0:
            carries = [first_block(hh, qq_first >= kk_first - N_META) for hh in range(2)]
            diag = range(1, k_per_q)
        else:
            carries = [first_block(hh, None) for hh in range(2)]
            for j in range(1, n_full):
                put_tile_scores(j + 1, q_slot)
                carries = [update(carries[hh], j, hh, None) for hh in range(2)]
            diag = range(k_per_q)
        for t in diag:
            j = n_full + t
            if t + 1 < k_per_q:
                put_tile_scores(j + 1, q_slot)
            elif i + 1 < n_q:
                put_queries(i + 1, 1 - q_slot)
                put_scores(sf_ref, 0, first, 1 - q_slot)
            mask = qq >= kk + t * ATTN_TILE
            carries = [update(carries[hh], j, hh, mask) for hh in range(2)]
        store(i, carries)


def _attention(q, k, v, qb, kb):
    bsz, seq, _ = q.shape
    n_tiles = (seq - N_META) // ATTN_TILE
    pair_blk = pl.BlockSpec((1, seq, LANES), lambda b, p: (b, 0, p))
    bias_blk = pl.BlockSpec((1, seq, LANES), lambda b, p: (b, 0, 0))
    return pl.pallas_call(
        _attn_kernel,
        grid=(bsz, D_ATTN // LANES),
        in_specs=[pair_blk, pair_blk, pair_blk, bias_blk, bias_blk],
        out_specs=pair_blk,
        out_shape=jax.ShapeDtypeStruct((bsz, seq, D_ATTN), bf16),
        scratch_shapes=[pltpu.VMEM((n_tiles, 2, HEAD_DIM + SUM_ROWS, ATTN_TILE), bf16),
                        pltpu.VMEM((2, HEAD_DIM + SUM_ROWS, LANES), bf16),
                        pltpu.VMEM((2, 2, 2 * LANES, ATTN_QUERIES), bf16),
                        pltpu.VMEM((2, 2, ATTN_TILE, ATTN_QUERIES), f32),
                        pltpu.VMEM((2, N_META + ATTN_TILE, ATTN_QUERIES), f32)],
        compiler_params=pltpu.CompilerParams(
            dimension_semantics=("arbitrary", "arbitrary"),
            vmem_limit_bytes=VMEM_LIMIT),
        name="attention",
    )(q, k, v, qb, kb)


def _head_norm_block(y, g):
    lo_lanes = lax.broadcasted_iota(jnp.int32, y.shape, 1) < HEAD_DIM
    y2 = y * y
    s_lo = jnp.sum(jnp.where(lo_lanes, y2, 0.0), axis=-1, keepdims=True)
    s_hi = jnp.sum(jnp.where(lo_lanes, 0.0, y2), axis=-1, keepdims=True)
    r_lo = lax.rsqrt(s_lo * (1.0 / HEAD_DIM) + RMS_EPS)
    r_hi = lax.rsqrt(s_hi * (1.0 / HEAD_DIM) + RMS_EPS)
    return y * jnp.where(lo_lanes, r_lo, r_hi) * g


def _mixer_out_kernel(h_ref, ya_ref, a_ref, cx_ref, sb_ref, dw_w_ref, dw_b_ref,
                      ln_g_ref, ln_b_ref, pw_ref, scw_ref, hg_ref, wo_ref, pg_ref,
                      o_ref, a_ext, cx_ext, a_sh, conv_ref, w_bc):
    tm = h_ref.shape[1]

    @pl.when(pl.program_id(1) == 0)
    def _():
        a_ext[0:CONV_HALO, :] = jnp.zeros((CONV_HALO, D_CONV), f32)
        cx_ext[0:SHORT_HALO, :] = jnp.zeros((SHORT_HALO, D_SC), f32)

    @pl.when(pl.program_id(1) > 0)
    def _():
        a_ext[0:CONV_HALO, :] = a_ext[tm:tm + CONV_HALO, :]
        cx_ext[0:SHORT_HALO, :] = cx_ext[tm:tm + SHORT_HALO, :]

    a_ext[CONV_HALO:CONV_HALO + tm, :] = a_ref[0].astype(f32)
    cx_ext[SHORT_HALO:SHORT_HALO + tm, :] = cx_ref[0].astype(f32)

    span = tm + CONV_HALO - SUBLANES
    for r in range(1, SUBLANES):
        a_sh[r - 1] = a_ext[r:r + span, :]
    for t in range(CONV_KERNEL):
        w_bc[t] = jnp.broadcast_to(dw_w_ref[t:t + 1, :], (SUBLANES, D_CONV))

    def conv_rows(rb, _):
        base = pl.multiple_of(rb * CONV_ROWS, CONV_ROWS)
        acc = jnp.broadcast_to(dw_b_ref[...], (CONV_ROWS, D_CONV))
        first_off = CONV_HALO - (CONV_KERNEL - 1)
        for r in range(SUBLANES):
            ms = [off // SUBLANES for off in range(first_off, CONV_HALO + 1)
                  if off % SUBLANES == r]
            rows = pl.ds(pl.multiple_of(base + ms[0] * SUBLANES, SUBLANES),
                         (ms[-1] - ms[0]) * SUBLANES + CONV_ROWS)
            window = a_ext[rows, :] if r == 0 else a_sh[r - 1, rows, :]
            for m in ms:
                t = m * SUBLANES + r - first_off
                lo = (m - ms[0]) * SUBLANES
                w = jnp.concatenate([w_bc[t]] * (CONV_ROWS // SUBLANES), axis=0)
                acc = acc + w * window[lo:lo + CONV_ROWS, :]
        conv_ref[pl.ds(base, CONV_ROWS), :] = acc
        return 0

    lax.fori_loop(0, tm // CONV_ROWS, conv_rows, 0)
    conv = conv_ref[...]
    mu = jnp.mean(conv, axis=-1, keepdims=True)
    cen = conv - mu
    var = jnp.mean(cen * cen, axis=-1, keepdims=True)
    ln = cen * lax.rsqrt(var + LN_EPS) * ln_g_ref[...] + ln_b_ref[...]
    y_conv = jnp.dot(_silu(ln).astype(bf16), pw_ref[...], preferred_element_type=f32)

    sc = None
    for t in range(SC_KERNEL):
        off = SHORT_HALO - (SC_KERNEL - 1) + t
        term = scw_ref[t:t + 1, :] * cx_ext[off:off + tm, :]
        sc = term if sc is None else sc + term
    y_sc = sb_ref[0].astype(f32) * sc

    blocks = []
    for c in range(D_MODEL // LANES):
        lo = c * LANES
        if lo < D_ATTN:
            yb = ya_ref[0, :, lo:lo + LANES].astype(f32)
        elif lo < D_ATTN + D_CONV:
            yb = y_conv[:, lo - D_ATTN:lo - D_ATTN + LANES]
        else:
            yb = y_sc[:, lo - D_ATTN - D_CONV:lo - D_ATTN - D_CONV + LANES]
        blocks.append(_head_norm_block(yb, hg_ref[:, lo:lo + LANES]).astype(bf16))
    yn = jnp.concatenate(blocks, axis=-1)
    y = jnp.dot(yn, wo_ref[...], preferred_element_type=f32)
    o_ref[0] = h_ref[0] + _rms_norm(y, pg_ref[...])


def _mixer_out(h, y_attn, a, cx, sb, dw_w, dw_b, ln_g, ln_b, pw, scw, head_g, w_out, post_g):
    bsz, seq, d = h.shape
    assert seq % SEQ_TILE == 0 and SEQ_TILE % N_META == 0
    n_tiles = seq // SEQ_TILE
    tile = lambda n: pl.BlockSpec((1, SEQ_TILE, n), lambda b, j: (b, j, 0))
    whole = lambda x: pl.BlockSpec(x.shape, lambda b, j: (0,) * x.ndim)
    params = (dw_w, dw_b, ln_g, ln_b, pw, scw, head_g, w_out, post_g)
    return pl.pallas_call(
        _mixer_out_kernel,
        grid=(bsz, n_tiles),
        in_specs=[tile(d), tile(D_ATTN), tile(D_CONV), tile(D_SC), tile(D_SC)]
                 + [whole(x) for x in params],
        out_specs=tile(d),
        out_shape=jax.ShapeDtypeStruct(h.shape, h.dtype),
        scratch_shapes=[pltpu.VMEM((CONV_HALO + SEQ_TILE, D_CONV), f32),
                        pltpu.VMEM((SHORT_HALO + SEQ_TILE, D_SC), f32),
                        pltpu.VMEM((SUBLANES - 1, SEQ_TILE + CONV_HALO - SUBLANES, D_CONV), f32),
                        pltpu.VMEM((SEQ_TILE, D_CONV), f32),
                        pltpu.VMEM((CONV_KERNEL, SUBLANES, D_CONV), f32)],
        compiler_params=pltpu.CompilerParams(
            dimension_semantics=("arbitrary", "arbitrary"),
            vmem_limit_bytes=VMEM_LIMIT),
        name="mixer_out",
    )(h, y_attn, a, cx, sb, *params)


def _ffn_kernel(h_ref, pre_g_ref, wg_ref, wu_ref, cw_ref, wd_ref, post_g_ref,
                o_ref, u_ref, z_ext, halo_ref, acc_ref, *out_scratch, drop_meta):
    tm = h_ref.shape[1]
    n_chunks = wg_ref.shape[0]
    fc = wg_ref.shape[2]

    @pl.when(pl.program_id(1) == 0)
    def _():
        halo_ref[...] = jnp.zeros_like(halo_ref)

    u_ref[...] = _rms_norm(h_ref[0], pre_g_ref[...]).astype(bf16)
    acc_ref[...] = jnp.zeros_like(acc_ref)

    def up(c, slot):
        u = u_ref[...]
        z_ext[slot, 0:SHORT_HALO, :] = halo_ref[c]
        z_ext[slot, SHORT_HALO:SHORT_HALO + tm, 0:fc] = jnp.dot(
            u, wg_ref[c], preferred_element_type=f32)
        z_ext[slot, SHORT_HALO:SHORT_HALO + tm, fc:2 * fc] = jnp.dot(
            u, wu_ref[c], preferred_element_type=f32)
        halo_ref[c] = z_ext[slot, tm:tm + SHORT_HALO, :]

    def down(c, slot):
        w = cw_ref[c]
        conv = None
        for t in range(FFN_KERNEL):
            off = SHORT_HALO - (FFN_KERNEL - 1) + t
            term = w[t:t + 1, :] * z_ext[slot, off:off + tm, :]
            conv = term if conv is None else conv + term
        act = (_silu(conv[:, 0:fc]) * conv[:, fc:2 * fc]).astype(bf16)
        acc_ref[...] += jnp.dot(act, wd_ref[c], preferred_element_type=f32)

    assert n_chunks % 2 == 1
    up(0, 0)

    def chunk_pair(k, _):
        c = 2 * k
        up(c + 1, 1)
        down(c, 0)
        up(c + 2, 0)
        down(c + 1, 1)
        return 0

    lax.fori_loop(0, n_chunks // 2, chunk_pair, 0)
    down(n_chunks - 1, 0)
    res = h_ref[0] + _rms_norm(acc_ref[...], post_g_ref[...])
    if not drop_meta:
        o_ref[0] = res
        return

    res_ref, sem = out_scratch
    b, j = pl.program_id(0), pl.program_id(1)
    step = b * pl.num_programs(1) + j
    last_step = pl.num_programs(0) * pl.num_programs(1) - 1

    def body_copy(jj):
        return pltpu.make_async_copy(
            res_ref.at[pl.ds(N_META, tm - N_META)],
            o_ref.at[b, pl.ds(jj * tm, tm - N_META)], sem.at[0])

    def head_copy(jj):
        return pltpu.make_async_copy(
            res_ref.at[pl.ds(0, N_META)],
            o_ref.at[b, pl.ds(jj * tm - N_META, N_META)], sem.at[1])

    @pl.when(step > 0)
    def _():
        body_copy(j).wait()

    @pl.when((step > 0) & (j != 1))
    def _():
        head_copy(jnp.maximum(j, 1)).wait()

    res_ref[...] = res
    body_copy(j).start()

    @pl.when(j > 0)
    def _():
        head_copy(j).start()

    @pl.when(step == last_step)
    def _():
        body_copy(j).wait()
        head_copy(j).wait()


def _ffn(h, pre_g, wg, wu, cw, wd, post_g, drop_meta=False):
    bsz, seq, d = h.shape
    assert seq % SEQ_TILE == 0 and SEQ_TILE % N_META == 0
    n_tiles = seq // SEQ_TILE
    n_chunks, _, fc = wg.shape
    tile = pl.BlockSpec((1, SEQ_TILE, d), lambda b, j: (b, j, 0))
    whole = lambda x: pl.BlockSpec(x.shape, lambda b, j: (0,) * x.ndim)
    params = (pre_g, wg, wu, cw, wd, post_g)
    if drop_meta:
        assert n_tiles >= 2
        out_spec = pl.BlockSpec(memory_space=pl.ANY)
        out_shape = jax.ShapeDtypeStruct((bsz, seq - N_META, d), h.dtype)
        out_scratch = [pltpu.VMEM((SEQ_TILE, d), f32), pltpu.SemaphoreType.DMA((2,))]
    else:
        out_spec, out_shape, out_scratch = tile, jax.ShapeDtypeStruct(h.shape, h.dtype), []
    return pl.pallas_call(
        functools.partial(_ffn_kernel, drop_meta=drop_meta),
        grid=(bsz, n_tiles),
        in_specs=[tile] + [whole(x) for x in params],
        out_specs=out_spec,
        out_shape=out_shape,
        scratch_shapes=[pltpu.VMEM((SEQ_TILE, d), bf16),
                        pltpu.VMEM((2, SHORT_HALO + SEQ_TILE, 2 * fc), f32),
                        pltpu.VMEM((n_chunks, SHORT_HALO, 2 * fc), f32),
                        pltpu.VMEM((SEQ_TILE, d), f32)] + out_scratch,
        compiler_params=pltpu.CompilerParams(
            dimension_semantics=("arbitrary", "arbitrary"),
            vmem_limit_bytes=VMEM_LIMIT),
        name="ffn",
    )(h, *params)


def _prep_w_in(w_in, b_forget):
    q, k, v, f, a_val, a_gate, sc_b, sc_c, sc_x = jnp.split(
        w_in, [512, 1024, 1536, 1544, 1800, 2056, 2312, 2568], axis=-1)
    pad = LANES - ATTN_HEADS * BIAS_LANES
    f_rep = jnp.pad(jnp.repeat(f, BIAS_LANES, axis=-1), ((0, 0), (0, pad)))
    w = jnp.concatenate([q, k, v, a_val, a_gate, sc_b, sc_c, sc_x, f_rep], axis=-1)
    bf = jnp.pad(jnp.repeat(b_forget, BIAS_LANES), (0, pad))[None, :]
    return w.astype(bf16), bf.astype(f32)


def _chunk_cols(w, n_chunks):
    k, n = w.shape
    return w.reshape(k, n_chunks, n // n_chunks).transpose(1, 0, 2)


def _prep_ffn(w_up, conv_w, w_down):
    n_chunks = D_FF // FF_CHUNK
    wg = _chunk_cols(w_up[:, :D_FF], n_chunks).astype(bf16)
    wu = _chunk_cols(w_up[:, D_FF:], n_chunks).astype(bf16)
    cg = _chunk_cols(conv_w[:, :D_FF], n_chunks)
    cu = _chunk_cols(conv_w[:, D_FF:], n_chunks)
    cw = jnp.concatenate([cg, cu], axis=-1)
    cw = jnp.pad(cw, ((0, 0), (0, SUBLANES - FFN_KERNEL), (0, 0)))
    wd = w_down.reshape(n_chunks, FF_CHUNK, D_MODEL).astype(bf16)
    return wg, wu, cw, wd


def kernel(x, meta_tokens, mix_pre_g, mix_post_g, w_in, b_forget, a_dw_w, a_dw_b, a_ln_g, a_ln_b, a_pw_w, sc_conv_w, head_g, w_out, ffn_pre_g, ffn_post_g, ffn_w_up, ffn_conv_w, ffn_w_down):
    depth = w_in.shape[0]
    row = lambda p: p[None, :]
    h = x
    for l in range(depth):
        w_in_r, bf_r = _prep_w_in(w_in[l], b_forget[l])
        if l == 0:
            h, q, k, v, a, cx, sb, qb, kb = _in_proj(
                x, row(mix_pre_g[l]), w_in_r, bf_r, meta=meta_tokens.astype(x.dtype))
        else:
            q, k, v, a, cx, sb, qb, kb = _in_proj(h, row(mix_pre_g[l]), w_in_r, bf_r)
        y_attn = _attention(q, k, v, qb, kb)
        h = _mixer_out(h, y_attn, a, cx, sb, a_dw_w[l], row(a_dw_b[l]), row(a_ln_g[l]),
                       row(a_ln_b[l]), a_pw_w[l].astype(bf16), sc_conv_w[l],
                       row(head_g[l]), w_out[l].astype(bf16), row(mix_post_g[l]))
        wg, wu, cw, wd = _prep_ffn(ffn_w_up[l], ffn_conv_w[l], ffn_w_down[l])
        h = _ffn(h, row(ffn_pre_g[l]), wg, wu, cw, wd, row(ffn_post_g[l]),
                 drop_meta=(l == depth - 1))
    return h
```

```python
import functools

import jax
import jax.numpy as jnp
from jax import lax
from jax.experimental import pallas as pl
from jax.experimental.pallas import tpu as pltpu

D_MODEL = 1024
N_META = 16
ATTN_HEADS = 8
HEAD_DIM = 64
D_ATTN = ATTN_HEADS * HEAD_DIM
D_CONV = 256
D_SC = 256
CONV_KERNEL = 31
SC_KERNEL = 3
D_FF = 2816
FFN_KERNEL = 3
RMS_EPS = 1e-6
LN_EPS = 1e-5
NEG_INF = -1e30
LOG2E = 1.4426950408889634

LANES = 128
SUBLANES = 8
BIAS_LANES = 6
SEQ_TILE = 688
ATTN_TILE = 256
ATTN_QUERIES = 512
SUM_ROWS = 16
CONV_HALO = 32
SHORT_HALO = 8
CONV_ROWS = 16
FF_CHUNK = 256
VMEM_LIMIT = 56 * 1024 * 1024

_OFF_Q = 0
_OFF_K = _OFF_Q + D_ATTN
_OFF_V = _OFF_K + D_ATTN
_OFF_AVAL = _OFF_V + D_ATTN
_OFF_AGATE = _OFF_AVAL + D_CONV
_OFF_SB = _OFF_AGATE + D_CONV
_OFF_SC = _OFF_SB + D_SC
_OFF_SX = _OFF_SC + D_SC
_OFF_F = _OFF_SX + D_SC
D_IN_PAD = _OFF_F + LANES

f32 = jnp.float32
bf16 = jnp.bfloat16


def _rms_norm(x, g):
    ms = jnp.mean(x * x, axis=-1, keepdims=True)
    return x * lax.rsqrt(ms + RMS_EPS) * g


def _silu(x):
    half = 0.5 * x
    return half + half * jnp.tanh(half)


def _cumsum_rows(x):
    n = x.shape[0]
    row = lax.broadcasted_iota(jnp.int32, x.shape, 0)
    shift = 1
    while shift < n:
        x = x + jnp.where(row >= shift, pltpu.roll(x, shift, axis=0), 0.0)
        shift *= 2
    return x


def _in_proj_kernel(*refs, prepend_meta):
    if prepend_meta:
        (x_ref, x_prev_ref, meta_ref, g_ref, w_ref, bf_ref, h0_ref, q_ref, k_ref, v_ref,
         a_ref, cx_ref, sb_ref, qb_ref, kb_ref, carry_ref) = refs
        tm = x_ref.shape[1]
        head = jnp.where(pl.program_id(1) == 0, meta_ref[...], x_prev_ref[0])
        h = jnp.concatenate([head, x_ref[0, 0:tm - N_META, :]], axis=0)
        h0_ref[0] = h
    else:
        (h_ref, g_ref, w_ref, bf_ref, q_ref, k_ref, v_ref,
         a_ref, cx_ref, sb_ref, qb_ref, kb_ref, carry_ref) = refs
        tm = h_ref.shape[1]
        h = h_ref[0]

    @pl.when(pl.program_id(1) == 0)
    def _():
        carry_ref[...] = jnp.zeros_like(carry_ref)

    u = _rms_norm(h, g_ref[...]).astype(bf16)

    def seg(lo, n):
        return jnp.dot(u, w_ref[:, lo:lo + n], preferred_element_type=f32)

    q_ref[0] = (seg(_OFF_Q, D_ATTN) * (HEAD_DIM ** -0.5 * LOG2E)).astype(bf16)
    k_ref[0] = seg(_OFF_K, D_ATTN).astype(bf16)
    v_ref[0] = seg(_OFF_V, D_ATTN).astype(bf16)
    a_ref[0] = (seg(_OFF_AVAL, D_CONV) * jax.nn.sigmoid(seg(_OFF_AGATE, D_CONV))).astype(bf16)
    sb_ref[0] = seg(_OFF_SB, D_SC).astype(bf16)
    cx_ref[0] = (seg(_OFF_SC, D_SC) * seg(_OFF_SX, D_SC)).astype(bf16)

    fl = seg(_OFF_F, LANES) + bf_ref[...]
    log_f = jnp.minimum(fl, 0.0) - jnp.log1p(jnp.exp(-jnp.abs(fl)))
    cum = _cumsum_rows(log_f) + carry_ref[0:1, :]
    carry_ref[...] = jnp.broadcast_to(cum[tm - 1:tm, :], carry_ref.shape)

    cum2 = cum * LOG2E
    hi = cum2.astype(bf16)
    r1 = cum2 - hi.astype(f32)
    mid = r1.astype(bf16)
    lo = (r1 - mid.astype(f32)).astype(bf16)
    lane = lax.broadcasted_iota(jnp.int32, cum.shape, 1)
    piece = lane % BIAS_LANES
    used = lane < ATTN_HEADS * BIAS_LANES
    pieces = jnp.where(piece % 3 == 0, hi, jnp.where(piece % 3 == 1, mid, lo)).astype(f32)
    qb = jnp.where(piece < 3, pieces, 1.0)
    kb = jnp.where(piece < 3, 1.0, -pieces)
    qb_ref[0] = jnp.where(used, qb, 0.0).astype(bf16)
    kb_ref[0] = jnp.where(used, kb, 0.0).astype(bf16)


def _in_proj(h, pre_g, w_in_r, bf_r, meta=None):
    prepend_meta = meta is not None
    bsz, rows, d = h.shape
    seq = rows + N_META if prepend_meta else rows
    assert seq % SEQ_TILE == 0 and SEQ_TILE % N_META == 0
    n_tiles = seq // SEQ_TILE
    tile = lambda n: pl.BlockSpec((1, SEQ_TILE, n), lambda b, j: (b, j, 0))
    whole = lambda a: pl.BlockSpec(a.shape, lambda b, j: (0,) * a.ndim)
    out = lambda n: jax.ShapeDtypeStruct((bsz, seq, n), bf16)
    in_specs = [tile(d), whole(pre_g), whole(w_in_r), whole(bf_r)]
    operands = [h, pre_g, w_in_r, bf_r]
    out_specs = [tile(D_ATTN), tile(D_ATTN), tile(D_ATTN), tile(D_CONV),
                 tile(D_SC), tile(D_SC), tile(LANES), tile(LANES)]
    out_shape = [out(D_ATTN), out(D_ATTN), out(D_ATTN), out(D_CONV),
                 out(D_SC), out(D_SC), out(LANES), out(LANES)]
    if prepend_meta:
        per_tile = SEQ_TILE // N_META
        prev_rows = pl.BlockSpec((1, N_META, d),
                                 lambda b, j: (b, jnp.maximum(j * per_tile - 1, 0), 0))
        in_specs = [tile(d), prev_rows, whole(meta)] + in_specs[1:]
        operands = [h, h, meta] + operands[1:]
        out_specs = [tile(d)] + out_specs
        out_shape = [jax.ShapeDtypeStruct((bsz, seq, d), h.dtype)] + out_shape
    return pl.pallas_call(
        functools.partial(_in_proj_kernel, prepend_meta=prepend_meta),
        grid=(bsz, n_tiles),
        in_specs=in_specs,
        out_specs=out_specs,
        out_shape=out_shape,
        scratch_shapes=[pltpu.VMEM((SUBLANES, LANES), f32)],
        compiler_params=pltpu.CompilerParams(
            dimension_semantics=("arbitrary", "arbitrary"),
            vmem_limit_bytes=VMEM_LIMIT),
        name="in_proj",
    )(*operands)


def _attn_kernel(q_ref, k_ref, v_ref, qb_ref, kb_ref, o_ref, vt_ref, vtm_ref,
                 qt_ref, st_ref, sf_ref):
    seq = q_ref.shape[1]
    n_tiles = (seq - N_META) // ATTN_TILE
    pair = pl.program_id(1)
    lane = lax.broadcasted_iota(jnp.int32, (1, LANES), 1)
    nt_dims = (((1,), (1,)), ((), ()))

    q_masks, b_masks = [], []
    for hh in range(2):
        head = 2 * pair + hh
        q_masks.append((lane // HEAD_DIM == hh).astype(bf16))
        b_masks.append(((lane >= BIAS_LANES * head) & (lane < BIAS_LANES * (head + 1))).astype(bf16))

    def queries(start, size, hh):
        return jnp.concatenate([q_ref[0, pl.ds(start, size), :] * q_masks[hh],
                                qb_ref[0, pl.ds(start, size), :] * b_masks[hh]], axis=1)

    def keys(start, size):
        return jnp.concatenate([k_ref[0, pl.ds(start, size), :],
                                kb_ref[0, pl.ds(start, size), :]], axis=1)

    def with_ones(vt_pair):
        ones = jnp.ones((SUM_ROWS, vt_pair.shape[1]), bf16)
        return [jnp.concatenate([vt_pair[hh * HEAD_DIM:(hh + 1) * HEAD_DIM], ones], axis=0)
                for hh in range(2)]

    v_meta = jnp.concatenate([v_ref[0, 0:N_META, :].astype(f32),
                              jnp.zeros((LANES - N_META, LANES), f32)], axis=0)
    for hh, vt_h in enumerate(with_ones(v_meta.T.astype(bf16))):
        vtm_ref[hh] = vt_h
    for j in range(n_tiles):
        lo = N_META + j * ATTN_TILE
        vt_pair = v_ref[0, lo:lo + ATTN_TILE, :].astype(f32).T.astype(bf16)
        for hh, vt_h in enumerate(with_ones(vt_pair)):
            vt_ref[j, hh] = vt_h

    lo_lanes = lax.broadcasted_iota(jnp.int32, (N_META, LANES), 1) < HEAD_DIM
    k_meta = keys(0, N_META)
    o_meta = []
    for hh in range(2):
        s = lax.dot_general(queries(0, N_META, hh), k_meta, nt_dims, preferred_element_type=f32)
        r = lax.broadcasted_iota(jnp.int32, s.shape, 0)
        c = lax.broadcasted_iota(jnp.int32, s.shape, 1)
        s = jnp.where(r >= c, s, NEG_INF)
        p = jnp.exp2(s - jnp.max(s, axis=-1, keepdims=True))
        o = jnp.dot(p.astype(bf16), v_ref[0, 0:N_META, :], preferred_element_type=f32)
        o_meta.append(o * (1.0 / jnp.sum(p, axis=-1, keepdims=True)))
    o_ref[0, 0:N_META, :] = jnp.where(lo_lanes, o_meta[0], o_meta[1]).astype(bf16)

    first = N_META + ATTN_TILE
    n_q = (seq - N_META) // ATTN_QUERIES
    k_per_q = ATTN_QUERIES // ATTN_TILE
    assert k_per_q >= 2 and ATTN_QUERIES % ATTN_TILE == 0 and (seq - N_META) % ATTN_QUERIES == 0

    def put_queries(i, slot):
        start = N_META + i * ATTN_QUERIES
        for hh in range(2):
            qt_ref[slot, hh] = queries(start, ATTN_QUERIES, hh).astype(f32).T.astype(bf16)

    def put_scores(dst, start, size, q_slot):
        kj = keys(start, size)
        for hh in range(2):
            dst[hh] = jnp.dot(kj, qt_ref[q_slot, hh], preferred_element_type=f32)

    def put_tile_scores(j, q_slot):
        put_scores(st_ref.at[j % 2], N_META + j * ATTN_TILE, ATTN_TILE, q_slot)

    def first_block(hh, mask):
        st = sf_ref[hh]
        if mask is not None:
            st = jnp.where(mask, st, NEG_INF)
        m = jnp.max(st, axis=0, keepdims=True)
        pt = jnp.exp2(st - m).astype(bf16)
        pt_meta = jnp.concatenate(
            [pt[0:N_META], jnp.zeros((LANES - N_META, ATTN_QUERIES), bf16)], axis=0)
        acc = (jnp.dot(vt_ref[0, hh], pt[N_META:first], preferred_element_type=f32)
               + jnp.dot(vtm_ref[hh], pt_meta, preferred_element_type=f32))
        return m, acc

    def update(carry, j, hh, mask):
        m, acc = carry
        st = st_ref[j % 2, hh]
        if mask is not None:
            st = jnp.where(mask, st, NEG_INF)
        m_new = jnp.maximum(m, jnp.max(st, axis=0, keepdims=True))
        alpha = jnp.exp2(m - m_new)
        pt = jnp.exp2(st - m_new).astype(bf16)
        acc = alpha * acc + jnp.dot(vt_ref[j, hh], pt, preferred_element_type=f32)
        return m_new, acc

    def store(i, carries):
        o_t = jnp.concatenate(
            [acc[0:HEAD_DIM] * (1.0 / acc[HEAD_DIM:HEAD_DIM + 1]) for _, acc in carries], axis=0)
        start = N_META + i * ATTN_QUERIES
        o_ref[0, start:start + ATTN_QUERIES, :] = o_t.T.astype(bf16)

    kk = lax.broadcasted_iota(jnp.int32, (ATTN_TILE, ATTN_QUERIES), 0)
    qq = lax.broadcasted_iota(jnp.int32, (ATTN_TILE, ATTN_QUERIES), 1)
    kk_first = lax.broadcasted_iota(jnp.int32, (first, ATTN_QUERIES), 0)
    qq_first = lax.broadcasted_iota(jnp.int32, (first, ATTN_QUERIES), 1)

    put_queries(0, 0)
    put_scores(sf_ref, 0, first, 0)
    for i in range(n_q):
        q_slot = i % 2
        n_full = i * k_per_q
        put_tile_scores(1, q_slot)
        if i == 0:
            carries = [first_block(hh, qq_first >= kk_first - N_META) for hh in range(2)]
            diag = range(1, k_per_q)
        else:
            carries = [first_block(hh, None) for hh in range(2)]
            for j in range(1, n_full):
                put_tile_scores(j + 1, q_slot)
                carries = [update(carries[hh], j, hh, None) for hh in range(2)]
            diag = range(k_per_q)
        for t in diag:
            j = n_full + t
            if t + 1 < k_per_q:
                put_tile_scores(j + 1, q_slot)
            elif i + 1 < n_q:
                put_queries(i + 1, 1 - q_slot)
                put_scores(sf_ref, 0, first, 1 - q_slot)
            mask = qq >= kk + t * ATTN_TILE
            carries = [update(carries[hh], j, hh, mask) for hh in range(2)]
        store(i, carries)


def _attention(q, k, v, qb, kb):
    bsz, seq, _ = q.shape
    n_tiles = (seq - N_META) // ATTN_TILE
    pair_blk = pl.BlockSpec((1, seq, LANES), lambda b, p: (b, 0, p))
    bias_blk = pl.BlockSpec((1, seq, LANES), lambda b, p: (b, 0, 0))
    return pl.pallas_call(
        _attn_kernel,
        grid=(bsz, D_ATTN // LANES),
        in_specs=[pair_blk, pair_blk, pair_blk, bias_blk, bias_blk],
        out_specs=pair_blk,
        out_shape=jax.ShapeDtypeStruct((bsz, seq, D_ATTN), bf16),
        scratch_shapes=[pltpu.VMEM((n_tiles, 2, HEAD_DIM + SUM_ROWS, ATTN_TILE), bf16),
                        pltpu.VMEM((2, HEAD_DIM + SUM_ROWS, LANES), bf16),
                        pltpu.VMEM((2, 2, 2 * LANES, ATTN_QUERIES), bf16),
                        pltpu.VMEM((2, 2, ATTN_TILE, ATTN_QUERIES), f32),
                        pltpu.VMEM((2, N_META + ATTN_TILE, ATTN_QUERIES), f32)],
        compiler_params=pltpu.CompilerParams(
            dimension_semantics=("arbitrary", "arbitrary"),
            vmem_limit_bytes=VMEM_LIMIT),
        name="attention",
    )(q, k, v, qb, kb)


def _head_norm_block(y, g):
    lo_lanes = lax.broadcasted_iota(jnp.int32, y.shape, 1) < HEAD_DIM
    y2 = y * y
    s_lo = jnp.sum(jnp.where(lo_lanes, y2, 0.0), axis=-1, keepdims=True)
    s_hi = jnp.sum(jnp.where(lo_lanes, 0.0, y2), axis=-1, keepdims=True)
    r_lo = lax.rsqrt(s_lo * (1.0 / HEAD_DIM) + RMS_EPS)
    r_hi = lax.rsqrt(s_hi * (1.0 / HEAD_DIM) + RMS_EPS)
    return y * jnp.where(lo_lanes, r_lo, r_hi) * g


def _mixer_out_kernel(h_ref, ya_ref, a_ref, cx_ref, sb_ref, dw_w_ref, dw_b_ref,
                      ln_g_ref, ln_b_ref, pw_ref, scw_ref, hg_ref, wo_ref, pg_ref,
                      o_ref, a_ext, cx_ext, a_sh, conv_ref, w_bc):
    tm = h_ref.shape[1]

    @pl.when(pl.program_id(1) == 0)
    def _():
        a_ext[0:CONV_HALO, :] = jnp.zeros((CONV_HALO, D_CONV), f32)
        cx_ext[0:SHORT_HALO, :] = jnp.zeros((SHORT_HALO, D_SC), f32)

    @pl.when(pl.program_id(1) > 0)
    def _():
        a_ext[0:CONV_HALO, :] = a_ext[tm:tm + CONV_HALO, :]
        cx_ext[0:SHORT_HALO, :] = cx_ext[tm:tm + SHORT_HALO, :]

    a_ext[CONV_HALO:CONV_HALO + tm, :] = a_ref[0].astype(f32)
    cx_ext[SHORT_HALO:SHORT_HALO + tm, :] = cx_ref[0].astype(f32)

    span = tm + CONV_HALO - SUBLANES
    for r in range(1, SUBLANES):
        a_sh[r - 1] = a_ext[r:r + span, :]
    for t in range(CONV_KERNEL):
        w_bc[t] = jnp.broadcast_to(dw_w_ref[t:t + 1, :], (SUBLANES, D_CONV))

    def conv_rows(rb, _):
        base = pl.multiple_of(rb * CONV_ROWS, CONV_ROWS)
        acc = jnp.broadcast_to(dw_b_ref[...], (CONV_ROWS, D_CONV))
        first_off = CONV_HALO - (CONV_KERNEL - 1)
        for r in range(SUBLANES):
            ms = [off // SUBLANES for off in range(first_off, CONV_HALO + 1)
                  if off % SUBLANES == r]
            rows = pl.ds(pl.multiple_of(base + ms[0] * SUBLANES, SUBLANES),
                         (ms[-1] - ms[0]) * SUBLANES + CONV_ROWS)
            window = a_ext[rows, :] if r == 0 else a_sh[r - 1, rows, :]
            for m in ms:
                t = m * SUBLANES + r - first_off
                lo = (m - ms[0]) * SUBLANES
                w = jnp.concatenate([w_bc[t]] * (CONV_ROWS // SUBLANES), axis=0)
                acc = acc + w * window[lo:lo + CONV_ROWS, :]
        conv_ref[pl.ds(base, CONV_ROWS), :] = acc
        return 0

    lax.fori_loop(0, tm // CONV_ROWS, conv_rows, 0)
    conv = conv_ref[...]
    mu = jnp.mean(conv, axis=-1, keepdims=True)
    cen = conv - mu
    var = jnp.mean(cen * cen, axis=-1, keepdims=True)
    ln = cen * lax.rsqrt(var + LN_EPS) * ln_g_ref[...] + ln_b_ref[...]
    y_conv = jnp.dot(_silu(ln).astype(bf16), pw_ref[...], preferred_element_type=f32)

    sc = None
    for t in range(SC_KERNEL):
        off = SHORT_HALO - (SC_KERNEL - 1) + t
        term = scw_ref[t:t + 1, :] * cx_ext[off:off + tm, :]
        sc = term if sc is None else sc + term
    y_sc = sb_ref[0].astype(f32) * sc

    blocks = []
    for c in range(D_MODEL // LANES):
        lo = c * LANES
        if lo < D_ATTN:
            yb = ya_ref[0, :, lo:lo + LANES].astype(f32)
        elif lo < D_ATTN + D_CONV:
            yb = y_conv[:, lo - D_ATTN:lo - D_ATTN + LANES]
        else:
            yb = y_sc[:, lo - D_ATTN - D_CONV:lo - D_ATTN - D_CONV + LANES]
        blocks.append(_head_norm_block(yb, hg_ref[:, lo:lo + LANES]).astype(bf16))
    yn = jnp.concatenate(blocks, axis=-1)
    y = jnp.dot(yn, wo_ref[...], preferred_element_type=f32)
    o_ref[0] = h_ref[0] + _rms_norm(y, pg_ref[...])


def _mixer_out(h, y_attn, a, cx, sb, dw_w, dw_b, ln_g, ln_b, pw, scw, head_g, w_out, post_g):
    bsz, seq, d = h.shape
    assert seq % SEQ_TILE == 0 and SEQ_TILE % N_META == 0
    n_tiles = seq // SEQ_TILE
    tile = lambda n: pl.BlockSpec((1, SEQ_TILE, n), lambda b, j: (b, j, 0))
    whole = lambda x: pl.BlockSpec(x.shape, lambda b, j: (0,) * x.ndim)
    params = (dw_w, dw_b, ln_g, ln_b, pw, scw, head_g, w_out, post_g)
    return pl.pallas_call(
        _mixer_out_kernel,
        grid=(bsz, n_tiles),
        in_specs=[tile(d), tile(D_ATTN), tile(D_CONV), tile(D_SC), tile(D_SC)]
                 + [whole(x) for x in params],
        out_specs=tile(d),
        out_shape=jax.ShapeDtypeStruct(h.shape, h.dtype),
        scratch_shapes=[pltpu.VMEM((CONV_HALO + SEQ_TILE, D_CONV), f32),
                        pltpu.VMEM((SHORT_HALO + SEQ_TILE, D_SC), f32),
                        pltpu.VMEM((SUBLANES - 1, SEQ_TILE + CONV_HALO - SUBLANES, D_CONV), f32),
                        pltpu.VMEM((SEQ_TILE, D_CONV), f32),
                        pltpu.VMEM((CONV_KERNEL, SUBLANES, D_CONV), f32)],
        compiler_params=pltpu.CompilerParams(
            dimension_semantics=("arbitrary", "arbitrary"),
            vmem_limit_bytes=VMEM_LIMIT),
        name="mixer_out",
    )(h, y_attn, a, cx, sb, *params)


def _ffn_kernel(h_ref, pre_g_ref, wg_ref, wu_ref, cw_ref, wd_ref, post_g_ref,
                o_ref, u_ref, z_ext, halo_ref, act_ref, *out_scratch, drop_meta):
    tm = h_ref.shape[1]
    n_chunks = wg_ref.shape[0]
    fc = wg_ref.shape[2]

    @pl.when(pl.program_id(1) == 0)
    def _():
        halo_ref[...] = jnp.zeros_like(halo_ref)

    u_first = _rms_norm(h_ref[0], pre_g_ref[...]).astype(bf16)
    u_ref[...] = u_first

    def up(c, slot, u=None):
        if u is None:
            u = u_ref[...]
        z_ext[slot, 0:SHORT_HALO, :] = halo_ref[c]
        z_ext[slot, SHORT_HALO:SHORT_HALO + tm, 0:fc] = jnp.dot(
            u, wg_ref[c], preferred_element_type=f32)
        z_ext[slot, SHORT_HALO:SHORT_HALO + tm, fc:2 * fc] = jnp.dot(
            u, wu_ref[c], preferred_element_type=f32)
        halo_ref[c] = z_ext[slot, tm:tm + SHORT_HALO, :]

    def down(c, slot):
        w = cw_ref[c]
        conv = None
        for t in range(FFN_KERNEL):
            off = SHORT_HALO - (FFN_KERNEL - 1) + t
            term = w[t:t + 1, :] * z_ext[slot, off:off + tm, :]
            conv = term if conv is None else conv + term
        act_ref[c] = (_silu(conv[:, 0:fc]) * conv[:, fc:2 * fc]).astype(bf16)

    assert n_chunks % 2 == 1
    up(0, 0, u_first)

    def chunk_pair(k, _):
        c = 2 * k
        up(c + 1, 1)
        down(c, 0)
        up(c + 2, 0)
        down(c + 1, 1)
        return 0

    lax.fori_loop(0, n_chunks // 2, chunk_pair, 0)
    down(n_chunks - 1, 0)
    y = None
    for c in range(n_chunks):
        part = jnp.dot(act_ref[c], wd_ref[c], preferred_element_type=f32)
        y = part if y is None else y + part
    res = h_ref[0] + _rms_norm(y, post_g_ref[...])
    if not drop_meta:
        o_ref[0] = res
        return

    res_ref, sem = out_scratch
    b, j = pl.program_id(0), pl.program_id(1)
    step = b * pl.num_programs(1) + j
    last_step = pl.num_programs(0) * pl.num_programs(1) - 1

    def body_copy(jj):
        return pltpu.make_async_copy(
            res_ref.at[pl.ds(N_META, tm - N_META)],
            o_ref.at[b, pl.ds(jj * tm, tm - N_META)], sem.at[0])

    def head_copy(jj):
        return pltpu.make_async_copy(
            res_ref.at[pl.ds(0, N_META)],
            o_ref.at[b, pl.ds(jj * tm - N_META, N_META)], sem.at[1])

    @pl.when(step > 0)
    def _():
        body_copy(j).wait()

    @pl.when((step > 0) & (j != 1))
    def _():
        head_copy(jnp.maximum(j, 1)).wait()

    res_ref[...] = res
    body_copy(j).start()

    @pl.when(j > 0)
    def _():
        head_copy(j).start()

    @pl.when(step == last_step)
    def _():
        body_copy(j).wait()
        head_copy(j).wait()


def _ffn(h, pre_g, wg, wu, cw, wd, post_g, drop_meta=False):
    bsz, seq, d = h.shape
    assert seq % SEQ_TILE == 0 and SEQ_TILE % N_META == 0
    n_tiles = seq // SEQ_TILE
    n_chunks, _, fc = wg.shape
    tile = pl.BlockSpec((1, SEQ_TILE, d), lambda b, j: (b, j, 0))
    whole = lambda x: pl.BlockSpec(x.shape, lambda b, j: (0,) * x.ndim)
    params = (pre_g, wg, wu, cw, wd, post_g)
    if drop_meta:
        assert n_tiles >= 2
        out_spec = pl.BlockSpec(memory_space=pl.ANY)
        out_shape = jax.ShapeDtypeStruct((bsz, seq - N_META, d), h.dtype)
        out_scratch = [pltpu.VMEM((SEQ_TILE, d), f32), pltpu.SemaphoreType.DMA((2,))]
    else:
        out_spec, out_shape, out_scratch = tile, jax.ShapeDtypeStruct(h.shape, h.dtype), []
    return pl.pallas_call(
        functools.partial(_ffn_kernel, drop_meta=drop_meta),
        grid=(bsz, n_tiles),
        in_specs=[tile] + [whole(x) for x in params],
        out_specs=out_spec,
        out_shape=out_shape,
        scratch_shapes=[pltpu.VMEM((SEQ_TILE, d), bf16),
                        pltpu.VMEM((2, SHORT_HALO + SEQ_TILE, 2 * fc), f32),
                        pltpu.VMEM((n_chunks, SHORT_HALO, 2 * fc), f32),
                        pltpu.VMEM((n_chunks, SEQ_TILE, fc), bf16)] + out_scratch,
        compiler_params=pltpu.CompilerParams(
            dimension_semantics=("arbitrary", "arbitrary"),
            vmem_limit_bytes=VMEM_LIMIT),
        name="ffn",
    )(h, *params)


def _prep_w_in(w_in, b_forget):
    q, k, v, f, a_val, a_gate, sc_b, sc_c, sc_x = jnp.split(
        w_in, [512, 1024, 1536, 1544, 1800, 2056, 2312, 2568], axis=-1)
    pad = LANES - ATTN_HEADS * BIAS_LANES
    f_rep = jnp.pad(jnp.repeat(f, BIAS_LANES, axis=-1), ((0, 0), (0, pad)))
    w = jnp.concatenate([q, k, v, a_val, a_gate, sc_b, sc_c, sc_x, f_rep], axis=-1)
    bf = jnp.pad(jnp.repeat(b_forget, BIAS_LANES), (0, pad))[None, :]
    return w.astype(bf16), bf.astype(f32)


def _chunk_cols(w, n_chunks):
    k, n = w.shape
    return w.reshape(k, n_chunks, n // n_chunks).transpose(1, 0, 2)


def _prep_ffn(w_up, conv_w, w_down):
    n_chunks = D_FF // FF_CHUNK
    wg = _chunk_cols(w_up[:, :D_FF], n_chunks).astype(bf16)
    wu = _chunk_cols(w_up[:, D_FF:], n_chunks).astype(bf16)
    cg = _chunk_cols(conv_w[:, :D_FF], n_chunks)
    cu = _chunk_cols(conv_w[:, D_FF:], n_chunks)
    cw = jnp.concatenate([cg, cu], axis=-1)
    cw = jnp.pad(cw, ((0, 0), (0, SUBLANES - FFN_KERNEL), (0, 0)))
    wd = w_down.reshape(n_chunks, FF_CHUNK, D_MODEL).astype(bf16)
    return wg, wu, cw, wd


def kernel(x, meta_tokens, mix_pre_g, mix_post_g, w_in, b_forget, a_dw_w, a_dw_b, a_ln_g, a_ln_b, a_pw_w, sc_conv_w, head_g, w_out, ffn_pre_g, ffn_post_g, ffn_w_up, ffn_conv_w, ffn_w_down):
    depth = w_in.shape[0]
    row = lambda p: p[None, :]
    h = x
    for l in range(depth):
        w_in_r, bf_r = _prep_w_in(w_in[l], b_forget[l])
        if l == 0:
            h, q, k, v, a, cx, sb, qb, kb = _in_proj(
                x, row(mix_pre_g[l]), w_in_r, bf_r, meta=meta_tokens.astype(x.dtype))
        else:
            q, k, v, a, cx, sb, qb, kb = _in_proj(h, row(mix_pre_g[l]), w_in_r, bf_r)
        y_attn = _attention(q, k, v, qb, kb)
        h = _mixer_out(h, y_attn, a, cx, sb, a_dw_w[l], row(a_dw_b[l]), row(a_ln_g[l]),
                       row(a_ln_b[l]), a_pw_w[l].astype(bf16), sc_conv_w[l],
                       row(head_g[l]), w_out[l].astype(bf16), row(mix_post_g[l]))
        wg, wu, cw, wd = _prep_ffn(ffn_w_up[l], ffn_conv_w[l], ffn_w_down[l])
        h = _ffn(h, row(ffn_pre_g[l]), wg, wu, cw, wd, row(ffn_post_g[l]),
                 drop_meta=(l == depth - 1))
    return h
```
